```python
import math
import jax, jax.numpy as jnp
from jax import lax
import numpy as np

D_MODEL = 1024
BATCH = 16
SEQ = 2048
DEPTH = 2

CTX_LEN = 256
GRID_W = 64
EPS = 1e-6
N_EVEN = (DEPTH + 1) // 2
N_ODD = DEPTH // 2
N_MOD = 6

HG_HEADS = 4
HG_DK = 128
HG_DV = 128
HG_WK = HG_HEADS * HG_DK
HG_WV = HG_HEADS * HG_DV
HG_CHUNK = 64
RG_W = 512
RG_BLOCKS = 8
RG_BW = RG_W // RG_BLOCKS
RG_C = 8.0
CONV_W = 4
EVEN_SPLITS = (HG_WK, HG_WK + HG_WV, 2 * HG_WK + HG_WV, 3 * HG_WK + HG_WV, 3 * HG_WK + 2 * HG_WV, 3 * HG_WK + 2 * HG_WV + RG_W)
EVEN_IN = 3 * HG_WK + 2 * HG_WV + 2 * RG_W
EVEN_OUT = HG_WV + RG_W

ATT_HQ = 12
ATT_HKV = 4
ATT_G = ATT_HQ // ATT_HKV
ATT_DH = 64
ATT_WQ = ATT_HQ * ATT_DH
ATT_WKV = ATT_HKV * ATT_DH
WINDOW = 128
ATT_BLOCK = 128
ROPE_BASE = 10000.0
FN_GROUPS = 4
FN_DIM = 64
FN_W = FN_GROUPS * FN_DIM
ODD_SPLITS = (ATT_WQ, ATT_WQ + ATT_WKV, ATT_WQ + 2 * ATT_WKV)
ODD_IN = ATT_WQ + 2 * ATT_WKV + FN_W
ODD_OUT = ATT_WQ + FN_W

PK_HEADS = 8
PK_NKEYS = 128
PK_EXPERTS = PK_NKEYS * PK_NKEYS
PK_DK = 256
PK_DKH = PK_DK // 2
PK_TOPK = 16
PK_CHUNK = 128

kernel_name = 'hybrid_hgrn2_rglru_swa_fnet_peer_block'


def rms_norm(x, g):
    xf = x.astype(jnp.float32)
    y = xf * lax.rsqrt(jnp.mean(xf * xf, axis=-1, keepdims=True) + EPS)
    return (y * g.astype(jnp.float32)).astype(x.dtype)


def modulate(h, shift, scale):
    return h * (1.0 + scale) + shift


def flip(a):
    return jnp.flip(a, axis=1)


def centred_dwconv(x, w, b):
    left = CONV_W // 2
    y = lax.conv_general_dilated(x, w.astype(x.dtype)[:, None, :], (1,), [(left, CONV_W - 1 - left)],
                                 dimension_numbers=('NWC', 'WIO', 'NWC'), feature_group_count=x.shape[-1])
    return y + b.astype(x.dtype)


def hgrn_gate(f_raw, lb):
    bsz, t_len, _ = f_raw.shape
    f = lb + (1.0 - lb) * jax.nn.sigmoid(f_raw.astype(jnp.float32))
    shp = (bsz, t_len, HG_HEADS, HG_DK)
    return (1.0 - f).reshape(shp), jnp.log(f).reshape(shp)


def hgrn_chunk_scan(q, k, v, log_f, s0):
    bsz, t_len, n_h, _ = q.shape
    n_chunks = t_len // HG_CHUNK

    def to_chunks(a):
        return a.reshape(bsz, n_chunks, HG_CHUNK, n_h, a.shape[-1]).transpose(1, 0, 3, 2, 4)

    lower = jnp.tril(jnp.ones((HG_CHUNK, HG_CHUNK), dtype=bool))[:, :, None]

    def step(state, inp):
        qc, kc, vc, gc = inp
        b = jnp.cumsum(gc, axis=2)
        o = jnp.einsum('bhtd,bhdv->bhtv', qc * jnp.exp(b), state)
        rel = jnp.where(lower, b[:, :, :, None, :] - b[:, :, None, :, :], -jnp.inf)
        att = jnp.einsum('bhtd,bhsd,bhtsd->bhts', qc, kc, jnp.exp(rel))
        o = o + jnp.einsum('bhts,bhsv->bhtv', att, vc)
        b_end = b[:, :, -1]
        state = jnp.exp(b_end)[..., None] * state + jnp.einsum(
            'bhsd,bhsv->bhdv', kc * jnp.exp(b_end[:, :, None] - b), vc)
        return state, o

    s_end, o = lax.scan(step, s0, (to_chunks(q), to_chunks(k), to_chunks(v), to_chunks(log_f)))
    return o.transpose(1, 0, 3, 2, 4).reshape(bsz, t_len, n_h, v.shape[-1]), s_end


def hgrn_bidir(q, v, k_f, lf_f, k_b, lf_b, s_f, s_b):
    o_f, s_f = hgrn_chunk_scan(q, k_f, v, lf_f, s_f)
    o_b, s_b = hgrn_chunk_scan(flip(q), flip(k_b), flip(v), flip(lf_b), s_b)
    return o_f + flip(o_b), s_f, s_b


def block_diag(x, w):
    bsz, t_len, _ = x.shape
    y = jnp.einsum('btnc,ncd->btnd', x.reshape(bsz, t_len, RG_BLOCKS, RG_BW), w)
    return y.reshape(bsz, t_len, RG_W)


def rglru_gates(xc, wa, ba, wx, bx, lam):
    r = jax.nn.sigmoid(block_diag(xc, wa) + ba)
    i = jax.nn.sigmoid(block_diag(xc, wx) + bx)
    log_a = -RG_C * r * jax.nn.softplus(-lam.astype(jnp.float32))
    u = jnp.sqrt(-jnp.expm1(2.0 * log_a)) * (i * xc)
    return log_a, u


def linear_scan(log_a, u, h0):
    a = jnp.exp(log_a)
    u = u.at[:, 0].add(a[:, 0] * h0)

    def combine(l, r):
        return l[0] * r[0], r[0] * l[1] + r[1]

    _, h = lax.associative_scan(combine, (a, u), axis=1)
    return h, h[:, -1]


def rglru_bidir(g_f, g_b, h_f, h_b):
    y_f, h_f = linear_scan(g_f[0], g_f[1], h_f)
    y_b, h_b = linear_scan(flip(g_b[0]), flip(g_b[1]), h_b)
    return y_f + flip(y_b), h_f, h_b


def even_mixer(hl, hc, w_in, w_out, lb, hg_gain, conv_w, conv_b, wa, ba, wx, bx, lam):
    def prepare(h):
        bsz, t_len, _ = h.shape
        q, i, f_fw, f_bw, g, xr, gate = jnp.split(h @ w_in, EVEN_SPLITS, axis=-1)
        q = jax.nn.silu(q.astype(jnp.float32)).reshape(bsz, t_len, HG_HEADS, HG_DK)
        v = i.astype(jnp.float32).reshape(bsz, t_len, HG_HEADS, HG_DV)
        k_f, lf_f = hgrn_gate(f_fw, lb[0])
        k_b, lf_b = hgrn_gate(f_bw, lb[1])
        xc = centred_dwconv(xr, conv_w, conv_b).astype(jnp.float32)
        g_f = rglru_gates(xc, wa[0], ba[0], wx[0], bx[0], lam[0])
        g_b = rglru_gates(xc, wa[1], ba[1], wx[1], bx[1], lam[1])
        return (q, v, k_f, lf_f, k_b, lf_b), (g_f, g_b), (g, gate)

    def merge(o_hg, h_rg, g, gate):
        bsz, t_len, _ = g.shape
        y_a = rms_norm(o_hg, hg_gain) * jax.nn.silu(g.astype(jnp.float32)).reshape(bsz, t_len, HG_HEADS, HG_DV)
        y_b = jax.nn.gelu(gate.astype(jnp.float32)) * h_rg
        y = jnp.concatenate([y_a.reshape(bsz, t_len, HG_WV), y_b], axis=-1)
        return y.astype(g.dtype) @ w_out

    hg_c, rg_c, gt_c = prepare(hc)
    hg_l, rg_l, gt_l = prepare(hl)
    bsz = hl.shape[0]
    s0 = jnp.zeros((bsz, HG_HEADS, HG_DK, HG_DV), jnp.float32)
    h0 = jnp.zeros((bsz, RG_W), jnp.float32)
    o_c, s_f, s_b = hgrn_bidir(*hg_c, s0, s0)
    r_c, h_f, h_b = rglru_bidir(*rg_c, h0, h0)
    o_l, _, _ = hgrn_bidir(*hg_l, s_f, s_b)
    r_l, _, _ = rglru_bidir(*rg_l, h_f, h_b)
    return merge(o_l, r_l, *gt_l), merge(o_c, r_c, *gt_c)


def axial_rope(t_len):
    rows = t_len // GRID_W
    row = jnp.repeat(jnp.arange(rows, dtype=jnp.float32), GRID_W)
    col = jnp.tile(jnp.arange(GRID_W, dtype=jnp.float32), rows)
    n_freq = ATT_DH // 4
    inv = ROPE_BASE ** (-jnp.arange(n_freq, dtype=jnp.float32) / n_freq)
    ang = jnp.concatenate([row[:, None] * inv, col[:, None] * inv], axis=-1)
    return jnp.cos(ang), jnp.sin(ang)


def apply_rope(x, cos, sin):
    xf = x.astype(jnp.float32)
    x1, x2 = jnp.split(xf, 2, axis=-1)
    cs, sn = cos[None, :, None, :], sin[None, :, None, :]
    return jnp.concatenate([x1 * cs - x2 * sn, x2 * cs + x1 * sn], axis=-1).astype(x.dtype)


def sink_attention(q, k, v, sink, mask):
    s = jnp.einsum('bqhgd,bkhd->bhgqk', q, k).astype(jnp.float32) * (ATT_DH ** -0.5)
    if mask is not None:
        s = jnp.where(mask, s, -jnp.inf)
    sk = jnp.broadcast_to(sink.astype(jnp.float32)[None, :, :, None, None], s.shape[:-1] + (1,))
    p = jax.nn.softmax(jnp.concatenate([sk, s], axis=-1), axis=-1)[..., 1:]
    return jnp.einsum('bhgqk,bkhd->bqhgd', p.astype(v.dtype), v)


def windowed_attention(q, k, v, kc, vc, sink):
    bsz, t_len = q.shape[:2]
    n_blk = t_len // ATT_BLOCK
    n_ctx = kc.shape[1]
    pad = ((0, 0), (ATT_BLOCK, ATT_BLOCK), (0, 0), (0, 0))
    kp, vp = jnp.pad(k, pad), jnp.pad(v, pad)
    qb = q.reshape(bsz, n_blk, ATT_BLOCK, ATT_HKV, ATT_G, ATT_DH).transpose(1, 0, 2, 3, 4, 5)
    offs_q = jnp.arange(ATT_BLOCK)
    offs_k = jnp.arange(3 * ATT_BLOCK)
    ctx_mask = jnp.ones((ATT_BLOCK, n_ctx), dtype=bool)

    def block(args):
        n, qn = args
        kn = lax.dynamic_slice_in_dim(kp, n * ATT_BLOCK, 3 * ATT_BLOCK, axis=1)
        vn = lax.dynamic_slice_in_dim(vp, n * ATT_BLOCK, 3 * ATT_BLOCK, axis=1)
        qpos = n * ATT_BLOCK + offs_q
        kpos = (n - 1) * ATT_BLOCK + offs_k
        local = (jnp.abs(qpos[:, None] - kpos[None, :]) <= WINDOW) & (kpos >= 0)[None, :] & (kpos < t_len)[None, :]
        mask = jnp.concatenate([ctx_mask, local], axis=1)
        return sink_attention(qn, jnp.concatenate([kc, kn], axis=1), jnp.concatenate([vc, vn], axis=1), sink, mask)

    o = lax.map(block, (jnp.arange(n_blk), qb))
    return o.transpose(1, 0, 2, 3, 4, 5).reshape(bsz, t_len, ATT_WQ)


def fourier_mix(z):
    bsz, t_len, _ = z.shape
    zz = z.astype(jnp.float32).reshape(bsz, t_len, FN_GROUPS, FN_DIM)
    y = jnp.fft.fft2(zz, axes=(1, 3), norm='ortho').real
    return y.reshape(bsz, t_len, FN_W).astype(z.dtype)


def odd_mixer(hl, hc, w_in, w_out, q_gain, k_gain, sink, need_ctx):
    bsz, t_len, _ = hl.shape
    n_ctx = hc.shape[1]
    ql, kl, vl, fl = jnp.split(hl @ w_in, ODD_SPLITS, axis=-1)
    cos, sin = axial_rope(t_len)
    ql = apply_rope(rms_norm(ql.reshape(bsz, t_len, ATT_HQ, ATT_DH), q_gain), cos, sin)
    kl = apply_rope(rms_norm(kl.reshape(bsz, t_len, ATT_HKV, ATT_DH), k_gain), cos, sin)
    vl = vl.reshape(bsz, t_len, ATT_HKV, ATT_DH)
    kc, vc = jnp.split(hc @ w_in[:, ATT_WQ:ATT_WQ + 2 * ATT_WKV], 2, axis=-1)
    kc = rms_norm(kc.reshape(bsz, n_ctx, ATT_HKV, ATT_DH), k_gain)
    vc = vc.reshape(bsz, n_ctx, ATT_HKV, ATT_DH)
    sink_g = sink.reshape(ATT_HKV, ATT_G)
    a_l = windowed_attention(ql, kl, vl, kc, vc, sink_g)
    y_l = jnp.concatenate([a_l, fourier_mix(fl)], axis=-1) @ w_out
    if not need_ctx:
        return y_l, None
    qc = rms_norm((hc @ w_in[:, :ATT_WQ]).reshape(bsz, n_ctx, ATT_HKV, ATT_G, ATT_DH), q_gain)
    fc = hc @ w_in[:, ATT_WQ + 2 * ATT_WKV:]
    a_c = sink_attention(qc, kc, vc, sink_g, None).reshape(bsz, n_ctx, ATT_WQ)
    y_c = jnp.concatenate([a_c, fourier_mix(fc)], axis=-1) @ w_out
    return y_l, y_c


def peer(h, w_q, b_q, sub_keys, u, v):
    n_tok, d = h.shape

    def chunk(xc):
        q = (xc @ w_q + b_q).astype(jnp.float32).reshape(PK_CHUNK, PK_HEADS, 2, PK_DKH)
        s = jnp.einsum('thpd,hpkd->thpk', q, sub_keys.astype(jnp.float32))
        sv, si = lax.top_k(s, PK_TOPK)
        cand = (sv[:, :, 0, :, None] + sv[:, :, 1, None, :]).reshape(PK_CHUNK, PK_HEADS, PK_TOPK * PK_TOPK)
        cidx = (si[:, :, 0, :, None] * PK_NKEYS + si[:, :, 1, None, :]).reshape(PK_CHUNK, PK_HEADS, PK_TOPK * PK_TOPK)
        best, pos = lax.top_k(cand, PK_TOPK)
        eidx = jnp.take_along_axis(cidx, pos, axis=-1)
        gate = jax.nn.softmax(best, axis=-1)
        ue = jnp.take(u, eidx, axis=0)
        ve = jnp.take(v, eidx, axis=0)
        act = jax.nn.gelu(jnp.einsum('td,thkd->thk', xc, ue).astype(jnp.float32)) * gate
        return jnp.einsum('thk,thkd->td', act.astype(xc.dtype), ve)

    out = lax.map(chunk, h.reshape(n_tok // PK_CHUNK, PK_CHUNK, d))
    return out.reshape(n_tok, d)


def setup_inputs(seed: int = 0) -> dict:
    key = jax.random.key(seed)
    keys = iter(jax.random.split(key, 32))
    D = D_MODEL

    def nrm(shape, std):
        return std * jax.random.normal(next(keys), shape, jnp.float32)

    x = nrm((BATCH, SEQ, D), 1.0)
    c = nrm((BATCH, D), 1.0)
    ctx = nrm((BATCH, CTX_LEN, D), 1.0)
    c_ctx = nrm((D,), 1.0)
    w_mod = nrm((DEPTH, D, N_MOD * D), 0.5 * D ** -0.5)
    b_mod = nrm((DEPTH, N_MOD * D), 0.02)
    g_mix = 1.0 + nrm((DEPTH, D), 0.02)
    g_ffn = 1.0 + nrm((DEPTH, D), 0.02)
    e_w_in = nrm((N_EVEN, D, EVEN_IN), D ** -0.5)
    e_w_out = nrm((N_EVEN, EVEN_OUT, D), EVEN_OUT ** -0.5)
    hg_lb_logits = nrm((2, N_EVEN + 1, HG_WK), 0.1)
    hg_gain = 1.0 + nrm((N_EVEN, HG_DV), 0.02)
    rg_conv_w = nrm((N_EVEN, CONV_W, RG_W), CONV_W ** -0.5)
    rg_conv_b = nrm((N_EVEN, RG_W), 0.02)
    rg_wa = nrm((N_EVEN, 2, RG_BLOCKS, RG_BW, RG_BW), RG_BW ** -0.5)
    rg_ba = nrm((N_EVEN, 2, RG_W), 0.02)
    rg_wx = nrm((N_EVEN, 2, RG_BLOCKS, RG_BW, RG_BW), RG_BW ** -0.5)
    rg_bx = nrm((N_EVEN, 2, RG_W), 0.02)
    a0 = jax.random.uniform(next(keys), (N_EVEN, 2, RG_W), jnp.float32, 0.9, 0.999)
    rg_lambda = jnp.log(a0) - jnp.log1p(-a0)
    o_w_in = nrm((N_ODD, D, ODD_IN), D ** -0.5)
    o_w_out = nrm((N_ODD, ODD_OUT, D), ODD_OUT ** -0.5)
    q_gain = 1.0 + nrm((N_ODD, ATT_DH), 0.02)
    k_gain = 1.0 + nrm((N_ODD, ATT_DH), 0.02)
    sinks = nrm((N_ODD, ATT_HQ), 0.5)
    p_wq = nrm((DEPTH, D, PK_HEADS * PK_DK), D ** -0.5)
    p_bq = nrm((DEPTH, PK_HEADS * PK_DK), 0.02)
    p_keys = nrm((DEPTH, PK_HEADS, 2, PK_NKEYS, PK_DKH), PK_DKH ** -0.5)
    p_u = nrm((DEPTH, PK_EXPERTS, D), D ** -0.5)
    p_v = nrm((DEPTH, PK_EXPERTS, D), 0.5)
    return {'x': x, 'c': c, 'ctx': ctx, 'c_ctx': c_ctx, 'w_mod': w_mod, 'b_mod': b_mod,
            'g_mix': g_mix, 'g_ffn': g_ffn, 'e_w_in': e_w_in, 'e_w_out': e_w_out,
            'hg_lb_logits': hg_lb_logits, 'hg_gain': hg_gain, 'rg_conv_w': rg_conv_w, 'rg_conv_b': rg_conv_b,
            'rg_wa': rg_wa, 'rg_ba': rg_ba, 'rg_wx': rg_wx, 'rg_bx': rg_bx, 'rg_lambda': rg_lambda,
            'o_w_in': o_w_in, 'o_w_out': o_w_out, 'q_gain': q_gain, 'k_gain': k_gain, 'sinks': sinks,
            'p_wq': p_wq, 'p_bq': p_bq, 'p_keys': p_keys, 'p_u': p_u, 'p_v': p_v}


def reference(x, c, ctx, c_ctx, w_mod, b_mod, g_mix, g_ffn, e_w_in, e_w_out, hg_lb_logits, hg_gain,
              rg_conv_w, rg_conv_b, rg_wa, rg_ba, rg_wx, rg_bx, rg_lambda, o_w_in, o_w_out, q_gain, k_gain,
              sinks, p_wq, p_bq, p_keys, p_u, p_v):
    lb_all = jnp.cumsum(jax.nn.softmax(hg_lb_logits.astype(jnp.float32), axis=1), axis=1)
    lat, cx = x, ctx
    for l in range(DEPTH):
        last = l == DEPTH - 1
        j = l // 2
        m_lat = jnp.split((jax.nn.silu(c) @ w_mod[l] + b_mod[l])[:, None, :], N_MOD, axis=-1)
        m_ctx = jnp.split(jax.nn.silu(c_ctx) @ w_mod[l] + b_mod[l], N_MOD, axis=-1)
        hl = modulate(rms_norm(lat, g_mix[l]), m_lat[0], m_lat[1])
        hc = modulate(rms_norm(cx, g_mix[l]), m_ctx[0], m_ctx[1])
        if l % 2 == 0:
            yl, yc = even_mixer(hl, hc, e_w_in[j], e_w_out[j], lb_all[:, j], hg_gain[j], rg_conv_w[j], rg_conv_b[j],
                                rg_wa[j], rg_ba[j], rg_wx[j], rg_bx[j], rg_lambda[j])
        else:
            yl, yc = odd_mixer(hl, hc, o_w_in[j], o_w_out[j], q_gain[j], k_gain[j], sinks[j], not last)
        lat = lat + m_lat[2] * yl
        hl2 = modulate(rms_norm(lat, g_ffn[l]), m_lat[3], m_lat[4])
        n_lat = hl2.shape[0] * hl2.shape[1]
        if last:
            ff = peer(hl2.reshape(n_lat, hl2.shape[-1]), p_wq[l], p_bq[l], p_keys[l], p_u[l], p_v[l])
            lat = lat + m_lat[5] * ff.reshape(lat.shape)
        else:
            cx = cx + m_ctx[2] * yc
            hc2 = modulate(rms_norm(cx, g_ffn[l]), m_ctx[3], m_ctx[4])
            tokens = jnp.concatenate([hl2.reshape(n_lat, hl2.shape[-1]), hc2.reshape(-1, hc2.shape[-1])], axis=0)
            ff = peer(tokens, p_wq[l], p_bq[l], p_keys[l], p_u[l], p_v[l])
            lat = lat + m_lat[5] * ff[:n_lat].reshape(lat.shape)
            cx = cx + m_ctx[5] * ff[n_lat:].reshape(cx.shape)
    return lat
```

```python
import functools
import math

import jax
import jax.numpy as jnp
import numpy as np
from jax import lax
from jax.experimental import pallas as pl
from jax.experimental.pallas import tpu as pltpu

D_MODEL = 1024
DEPTH = 2
GRID_W = 64
EPS = 1e-6
N_MOD = 6

HG_HEADS = 4
HG_DK = 128
HG_DV = 128
HG_WK = HG_HEADS * HG_DK
HG_WV = HG_HEADS * HG_DV
HG_CHUNK = 64
RG_W = 512
RG_BLOCKS = 8
RG_BW = RG_W // RG_BLOCKS
RG_C = 8.0
CONV_W = 4
EVEN_SPLITS = (HG_WK, HG_WK + HG_WV, 2 * HG_WK + HG_WV, 3 * HG_WK + HG_WV, 3 * HG_WK + 2 * HG_WV,
               3 * HG_WK + 2 * HG_WV + RG_W)

ATT_HQ = 12
ATT_HKV = 4
ATT_G = ATT_HQ // ATT_HKV
ATT_DH = 64
ATT_WQ = ATT_HQ * ATT_DH
ATT_WKV = ATT_HKV * ATT_DH
WINDOW = 128
ATT_BLOCK = 128
ROPE_BASE = 10000.0
FN_GROUPS = 4
FN_DIM = 64
FN_W = FN_GROUPS * FN_DIM
ODD_SPLITS = (ATT_WQ, ATT_WQ + ATT_WKV, ATT_WQ + 2 * ATT_WKV)

PK_HEADS = 8
PK_NKEYS = 128
PK_DK = 256
PK_DKH = PK_DK // 2
PK_TOPK = 16
PK_CHUNK = 128


def _mm_kernel(a_ref, b_ref, o_ref):
    o_ref[...] = jnp.dot(a_ref[...].astype(jnp.bfloat16), b_ref[...].astype(jnp.bfloat16),
                         preferred_element_type=jnp.float32)


def pmm(a, b, tm=512, tn=512):
    m, k = a.shape
    n = b.shape[1]
    tm = next(t for t in (tm, 256, 128, m) if m % t == 0)
    tn = next(t for t in (tn, 256, 128, n) if n % t == 0)
    return pl.pallas_call(
        _mm_kernel,
        out_shape=jax.ShapeDtypeStruct((m, n), jnp.float32),
        grid=(m // tm, n // tn),
        in_specs=[pl.BlockSpec((tm, k), lambda i, j: (i, 0)),
                  pl.BlockSpec((k, tn), lambda i, j: (0, j))],
        out_specs=pl.BlockSpec((tm, tn), lambda i, j: (i, j)),
        compiler_params=pltpu.CompilerParams(dimension_semantics=("parallel", "parallel")),
        name="pmm",
    )(a, b)


def pmm3(a, b):
    bsz, t, k = a.shape
    return pmm(a.reshape(bsz * t, k), b).reshape(bsz, t, b.shape[1])


def rms_norm(x, g):
    xf = x.astype(jnp.float32)
    y = xf * lax.rsqrt(jnp.mean(xf * xf, axis=-1, keepdims=True) + EPS)
    return (y * g.astype(jnp.float32)).astype(x.dtype)


def modulate(h, shift, scale):
    return h * (1.0 + scale) + shift


def flip(a):
    return jnp.flip(a, axis=1)


def centred_dwconv(x, w, b):
    left = CONV_W // 2
    y = lax.conv_general_dilated(x, w.astype(x.dtype)[:, None, :], (1,), [(left, CONV_W - 1 - left)],
                                 dimension_numbers=('NWC', 'WIO', 'NWC'), feature_group_count=x.shape[-1])
    return y + b.astype(x.dtype)


def hgrn_gate(f_raw, lb):
    bsz, t_len, _ = f_raw.shape
    f = lb + (1.0 - lb) * jax.nn.sigmoid(f_raw.astype(jnp.float32))
    shp = (bsz, t_len, HG_HEADS, HG_DK)
    return (1.0 - f).reshape(shp), jnp.log(f).reshape(shp)


def hgrn_chunk_scan(q, k, v, log_f, s0):
    bsz, t_len, n_h, _ = q.shape
    n_chunks = t_len // HG_CHUNK

    def to_chunks(a):
        return a.reshape(bsz, n_chunks, HG_CHUNK, n_h, a.shape[-1]).transpose(1, 0, 3, 2, 4)

    lower = jnp.tril(jnp.ones((HG_CHUNK, HG_CHUNK), dtype=bool))[:, :, None]

    def step(state, inp):
        qc, kc, vc, gc = inp
        b = jnp.cumsum(gc, axis=2)
        o = jnp.einsum('bhtd,bhdv->bhtv', qc * jnp.exp(b), state)
        rel = jnp.where(lower, b[:, :, :, None, :] - b[:, :, None, :, :], -jnp.inf)
        att = jnp.einsum('bhtd,bhsd,bhtsd->bhts', qc, kc, jnp.exp(rel))
        o = o + jnp.einsum('bhts,bhsv->bhtv', att, vc)
        b_end = b[:, :, -1]
        state = jnp.exp(b_end)[..., None] * state + jnp.einsum(
            'bhsd,bhsv->bhdv', kc * jnp.exp(b_end[:, :, None] - b), vc)
        return state, o

    s_end, o = lax.scan(step, s0, (to_chunks(q), to_chunks(k), to_chunks(v), to_chunks(log_f)))
    return o.transpose(1, 0, 3, 2, 4).reshape(bsz, t_len, n_h, v.shape[-1]), s_end


def hgrn_bidir(q, v, k_f, lf_f, k_b, lf_b, s_f, s_b):
    o_f, s_f = hgrn_chunk_scan(q, k_f, v, lf_f, s_f)
    o_b, s_b = hgrn_chunk_scan(flip(q), flip(k_b), flip(v), flip(lf_b), s_b)
    return o_f + flip(o_b), s_f, s_b


def block_diag(x, w):
    bsz, t_len, _ = x.shape
    y = jnp.einsum('btnc,ncd->btnd', x.reshape(bsz, t_len, RG_BLOCKS, RG_BW), w)
    return y.reshape(bsz, t_len, RG_W)


def rglru_gates(xc, wa, ba, wx, bx, lam):
    r = jax.nn.sigmoid(block_diag(xc, wa) + ba)
    i = jax.nn.sigmoid(block_diag(xc, wx) + bx)
    log_a = -RG_C * r * jax.nn.softplus(-lam.astype(jnp.float32))
    u = jnp.sqrt(-jnp.expm1(2.0 * log_a)) * (i * xc)
    return log_a, u


def linear_scan(log_a, u, h0):
    a = jnp.exp(log_a)
    u = u.at[:, 0].add(a[:, 0] * h0)

    def combine(l, r):
        return l[0] * r[0], r[0] * l[1] + r[1]

    _, h = lax.associative_scan(combine, (a, u), axis=1)
    return h, h[:, -1]


def rglru_bidir(g_f, g_b, h_f, h_b):
    y_f, h_f = linear_scan(g_f[0], g_f[1], h_f)
    y_b, h_b = linear_scan(flip(g_b[0]), flip(g_b[1]), h_b)
    return y_f + flip(y_b), h_f, h_b


def even_mixer(hl, hc, w_in, w_out, lb, hg_gain, conv_w, conv_b, wa, ba, wx, bx, lam):
    def prepare(h):
        bsz, t_len, _ = h.shape
        q, i, f_fw, f_bw, g, xr, gate = jnp.split(pmm3(h, w_in), EVEN_SPLITS, axis=-1)
        q = jax.nn.silu(q.astype(jnp.float32)).reshape(bsz, t_len, HG_HEADS, HG_DK)
        v = i.astype(jnp.float32).reshape(bsz, t_len, HG_HEADS, HG_DV)
        k_f, lf_f = hgrn_gate(f_fw, lb[0])
        k_b, lf_b = hgrn_gate(f_bw, lb[1])
        xc = centred_dwconv(xr, conv_w, conv_b).astype(jnp.float32)
        g_f = rglru_gates(xc, wa[0], ba[0], wx[0], bx[0], lam[0])
        g_b = rglru_gates(xc, wa[1], ba[1], wx[1], bx[1], lam[1])
        return (q, v, k_f, lf_f, k_b, lf_b), (g_f, g_b), (g, gate)

    def merge(o_hg, h_rg, g, gate):
        bsz, t_len, _ = g.shape
        y_a = rms_norm(o_hg, hg_gain) * jax.nn.silu(g.astype(jnp.float32)).reshape(bsz, t_len, HG_HEADS, HG_DV)
        y_b = jax.nn.gelu(gate.astype(jnp.float32)) * h_rg
        y = jnp.concatenate([y_a.reshape(bsz, t_len, HG_WV), y_b], axis=-1)
        return pmm3(y.astype(g.dtype), w_out)

    hg_c, rg_c, gt_c = prepare(hc)
    hg_l, rg_l, gt_l = prepare(hl)
    bsz = hl.shape[0]
    s0 = jnp.zeros((bsz, HG_HEADS, HG_DK, HG_DV), jnp.float32)
    h0 = jnp.zeros((bsz, RG_W), jnp.float32)
    o_c, s_f, s_b = hgrn_bidir(*hg_c, s0, s0)
    r_c, h_f, h_b = rglru_bidir(*rg_c, h0, h0)
    o_l, _, _ = hgrn_bidir(*hg_l, s_f, s_b)
    r_l, _, _ = rglru_bidir(*rg_l, h_f, h_b)
    return merge(o_l, r_l, *gt_l), merge(o_c, r_c, *gt_c)


def axial_rope(t_len):
    rows = t_len // GRID_W
    row = jnp.repeat(jnp.arange(rows, dtype=jnp.float32), GRID_W)
    col = jnp.tile(jnp.arange(GRID_W, dtype=jnp.float32), rows)
    n_freq = ATT_DH // 4
    inv = ROPE_BASE ** (-jnp.arange(n_freq, dtype=jnp.float32) / n_freq)
    ang = jnp.concatenate([row[:, None] * inv, col[:, None] * inv], axis=-1)
    return jnp.cos(ang), jnp.sin(ang)


def apply_rope(x, cos, sin):
    xf = x.astype(jnp.float32)
    x1, x2 = jnp.split(xf, 2, axis=-1)
    cs, sn = cos[None, :, None, :], sin[None, :, None, :]
    return jnp.concatenate([x1 * cs - x2 * sn, x2 * cs + x1 * sn], axis=-1).astype(x.dtype)


def sink_attention(q, k, v, sink, mask):
    s = jnp.einsum('bqhgd,bkhd->bhgqk', q, k).astype(jnp.float32) * (ATT_DH ** -0.5)
    if mask is not None:
        s = jnp.where(mask, s, -jnp.inf)
    sk = jnp.broadcast_to(sink.astype(jnp.float32)[None, :, :, None, None], s.shape[:-1] + (1,))
    p = jax.nn.softmax(jnp.concatenate([sk, s], axis=-1), axis=-1)[..., 1:]
    return jnp.einsum('bhgqk,bkhd->bqhgd', p.astype(v.dtype), v)


def windowed_attention(q, k, v, kc, vc, sink):
    bsz, t_len = q.shape[:2]
    n_blk = t_len // ATT_BLOCK
    n_ctx = kc.shape[1]
    pad = ((0, 0), (ATT_BLOCK, ATT_BLOCK), (0, 0), (0, 0))
    kp, vp = jnp.pad(k, pad), jnp.pad(v, pad)
    qb = q.reshape(bsz, n_blk, ATT_BLOCK, ATT_HKV, ATT_G, ATT_DH).transpose(1, 0, 2, 3, 4, 5)
    offs_q = jnp.arange(ATT_BLOCK)
    offs_k = jnp.arange(3 * ATT_BLOCK)
    ctx_mask = jnp.ones((ATT_BLOCK, n_ctx), dtype=bool)

    def block(args):
        n, qn = args
        kn = lax.dynamic_slice_in_dim(kp, n * ATT_BLOCK, 3 * ATT_BLOCK, axis=1)
        vn = lax.dynamic_slice_in_dim(vp, n * ATT_BLOCK, 3 * ATT_BLOCK, axis=1)
        qpos = n * ATT_BLOCK + offs_q
        kpos = (n - 1) * ATT_BLOCK + offs_k
        local = (jnp.abs(qpos[:, None] - kpos[None, :]) <= WINDOW) & (kpos >= 0)[None, :] & (kpos < t_len)[None, :]
        mask = jnp.concatenate([ctx_mask, local], axis=1)
        return sink_attention(qn, jnp.concatenate([kc, kn], axis=1), jnp.concatenate([vc, vn], axis=1), sink, mask)

    o = lax.map(block, (jnp.arange(n_blk), qb))
    return o.transpose(1, 0, 2, 3, 4, 5).reshape(bsz, t_len, ATT_WQ)


def fourier_mix(z):
    bsz, t_len, _ = z.shape
    zz = z.astype(jnp.float32).reshape(bsz, t_len, FN_GROUPS, FN_DIM)
    y = jnp.fft.fft2(zz, axes=(1, 3), norm='ortho').real
    return y.reshape(bsz, t_len, FN_W).astype(z.dtype)


def odd_mixer(hl, hc, w_in, w_out, q_gain, k_gain, sink, need_ctx):
    bsz, t_len, _ = hl.shape
    n_ctx = hc.shape[1]
    ql, kl, vl, fl = jnp.split(pmm3(hl, w_in), ODD_SPLITS, axis=-1)
    cos, sin = axial_rope(t_len)
    ql = apply_rope(rms_norm(ql.reshape(bsz, t_len, ATT_HQ, ATT_DH), q_gain), cos, sin)
    kl = apply_rope(rms_norm(kl.reshape(bsz, t_len, ATT_HKV, ATT_DH), k_gain), cos, sin)
    vl = vl.reshape(bsz, t_len, ATT_HKV, ATT_DH)
    kc, vc = jnp.split(pmm3(hc, w_in[:, ATT_WQ:ATT_WQ + 2 * ATT_WKV]), 2, axis=-1)
    kc = rms_norm(kc.reshape(bsz, n_ctx, ATT_HKV, ATT_DH), k_gain)
    vc = vc.reshape(bsz, n_ctx, ATT_HKV, ATT_DH)
    sink_g = sink.reshape(ATT_HKV, ATT_G)
    a_l = windowed_attention(ql.reshape(bsz, t_len, ATT_HKV, ATT_G, ATT_DH), kl, vl, kc, vc, sink_g)
    y_l = pmm3(jnp.concatenate([a_l, fourier_mix(fl)], axis=-1), w_out)
    if not need_ctx:
        return y_l, None
    qc = rms_norm(pmm3(hc, w_in[:, :ATT_WQ]).reshape(bsz, n_ctx, ATT_HKV, ATT_G, ATT_DH), q_gain)
    fc = pmm3(hc, w_in[:, ATT_WQ + 2 * ATT_WKV:])
    a_c = sink_attention(qc, kc, vc, sink_g, None).reshape(bsz, n_ctx, ATT_WQ)
    y_c = pmm3(jnp.concatenate([a_c, fourier_mix(fc)], axis=-1), w_out)
    return y_l, y_c


def peer(h, w_q, b_q, sub_keys, u, v):
    n_tok, d = h.shape
    qall = pmm(h, w_q) + b_q

    def chunk(args):
        xc, q = args
        q = q.astype(jnp.float32).reshape(PK_CHUNK, PK_HEADS, 2, PK_DKH)
        s = jnp.einsum('thpd,hpkd->thpk', q, sub_keys.astype(jnp.float32))
        sv, si = lax.top_k(s, PK_TOPK)
        cand = (sv[:, :, 0, :, None] + sv[:, :, 1, None, :]).reshape(PK_CHUNK, PK_HEADS, PK_TOPK * PK_TOPK)
        cidx = (si[:, :, 0, :, None] * PK_NKEYS + si[:, :, 1, None, :]).reshape(PK_CHUNK, PK_HEADS, PK_TOPK * PK_TOPK)
        best, pos = lax.top_k(cand, PK_TOPK)
        eidx = jnp.take_along_axis(cidx, pos, axis=-1)
        gate = jax.nn.softmax(best, axis=-1)
        ue = jnp.take(u, eidx, axis=0)
        ve = jnp.take(v, eidx, axis=0)
        act = jax.nn.gelu(jnp.einsum('td,thkd->thk', xc, ue).astype(jnp.float32)) * gate
        return jnp.einsum('thk,thkd->td', act.astype(xc.dtype), ve)

    out = lax.map(chunk, (h.reshape(n_tok // PK_CHUNK, PK_CHUNK, d),
                          qall.reshape(n_tok // PK_CHUNK, PK_CHUNK, PK_HEADS * PK_DK)))
    return out.reshape(n_tok, d)


def kernel(x, c, ctx, c_ctx, w_mod, b_mod, g_mix, g_ffn, e_w_in, e_w_out, hg_lb_logits, hg_gain, rg_conv_w,
           rg_conv_b, rg_wa, rg_ba, rg_wx, rg_bx, rg_lambda, o_w_in, o_w_out, q_gain, k_gain, sinks, p_wq,
           p_bq, p_keys, p_u, p_v):
    lb_all = jnp.cumsum(jax.nn.softmax(hg_lb_logits.astype(jnp.float32), axis=1), axis=1)
    lat, cx = x, ctx
    for l in range(DEPTH):
        last = l == DEPTH - 1
        j = l // 2
        m_lat = jnp.split((jax.nn.silu(c) @ w_mod[l] + b_mod[l])[:, None, :], N_MOD, axis=-1)
        m_ctx = jnp.split(jax.nn.silu(c_ctx) @ w_mod[l] + b_mod[l], N_MOD, axis=-1)
        hl = modulate(rms_norm(lat, g_mix[l]), m_lat[0], m_lat[1])
        hc = modulate(rms_norm(cx, g_mix[l]), m_ctx[0], m_ctx[1])
        if l % 2 == 0:
            yl, yc = even_mixer(hl, hc, e_w_in[j], e_w_out[j], lb_all[:, j], hg_gain[j], rg_conv_w[j], rg_conv_b[j],
                                rg_wa[j], rg_ba[j], rg_wx[j], rg_bx[j], rg_lambda[j])
        else:
            yl, yc = odd_mixer(hl, hc, o_w_in[j], o_w_out[j], q_gain[j], k_gain[j], sinks[j], not last)
        lat = lat + m_lat[2] * yl
        hl2 = modulate(rms_norm(lat, g_ffn[l]), m_lat[3], m_lat[4])
        n_lat = hl2.shape[0] * hl2.shape[1]
        if last:
            ff = peer(hl2.reshape(n_lat, hl2.shape[-1]), p_wq[l], p_bq[l], p_keys[l], p_u[l], p_v[l])
            lat = lat + m_lat[5] * ff.reshape(lat.shape)
        else:
            cx = cx + m_ctx[2] * yc
            hc2 = modulate(rms_norm(cx, g_ffn[l]), m_ctx[3], m_ctx[4])
            tokens = jnp.concatenate([hl2.reshape(n_lat, hl2.shape[-1]), hc2.reshape(-1, hc2.shape[-1])], axis=0)
            ff = peer(tokens, p_wq[l], p_bq[l], p_keys[l], p_u[l], p_v[l])
            lat = lat + m_lat[5] * ff[:n_lat].reshape(lat.shape)
            cx = cx + m_ctx[5] * ff[n_lat:].reshape(cx.shape)
    return lat
```

```python
import functools
import math

import jax
import jax.numpy as jnp
import numpy as np
from jax import lax
from jax.experimental import pallas as pl
from jax.experimental.pallas import tpu as pltpu
from jax.experimental.pallas import tpu_sc as plsc

D_MODEL = 1024
DEPTH = 2
GRID_W = 64
EPS = 1e-6
N_MOD = 6

HG_HEADS = 4
HG_DK = 128
HG_DV = 128
HG_WK = HG_HEADS * HG_DK
HG_WV = HG_HEADS * HG_DV
HG_CHUNK = 64
RG_W = 512
RG_BLOCKS = 8
RG_BW = RG_W // RG_BLOCKS
RG_C = 8.0
CONV_W = 4
EVEN_SPLITS = (HG_WK, HG_WK + HG_WV, 2 * HG_WK + HG_WV, 3 * HG_WK + HG_WV, 3 * HG_WK + 2 * HG_WV,
               3 * HG_WK + 2 * HG_WV + RG_W)

ATT_HQ = 12
ATT_HKV = 4
ATT_G = ATT_HQ // ATT_HKV
ATT_DH = 64
ATT_WQ = ATT_HQ * ATT_DH
ATT_WKV = ATT_HKV * ATT_DH
WINDOW = 128
ATT_BLOCK = 128
ROPE_BASE = 10000.0
FN_GROUPS = 4
FN_DIM = 64
FN_W = FN_GROUPS * FN_DIM
ODD_SPLITS = (ATT_WQ, ATT_WQ + ATT_WKV, ATT_WQ + 2 * ATT_WKV)

PK_HEADS = 8
PK_NKEYS = 128
PK_DK = 256
PK_DKH = PK_DK // 2
PK_TOPK = 16
PK_CHUNK = 128


def _mm_kernel(a_ref, b_ref, o_ref):
    o_ref[...] = jnp.dot(a_ref[...].astype(jnp.bfloat16), b_ref[...].astype(jnp.bfloat16),
                         preferred_element_type=jnp.float32)


def pmm(a, b, tm=512, tn=512):
    m, k = a.shape
    n = b.shape[1]
    tm = next(t for t in (tm, 256, 128, m) if m % t == 0)
    tn = next(t for t in (tn, 256, 128, n) if n % t == 0)
    return pl.pallas_call(
        _mm_kernel,
        out_shape=jax.ShapeDtypeStruct((m, n), jnp.float32),
        grid=(m // tm, n // tn),
        in_specs=[pl.BlockSpec((tm, k), lambda i, j: (i, 0)),
                  pl.BlockSpec((k, tn), lambda i, j: (0, j))],
        out_specs=pl.BlockSpec((tm, tn), lambda i, j: (i, j)),
        compiler_params=pltpu.CompilerParams(dimension_semantics=("parallel", "parallel")),
        name="pmm",
    )(a, b)


def pmm3(a, b):
    bsz, t, k = a.shape
    return pmm(a.reshape(bsz * t, k), b).reshape(bsz, t, b.shape[1])


def rms_norm(x, g):
    xf = x.astype(jnp.float32)
    y = xf * lax.rsqrt(jnp.mean(xf * xf, axis=-1, keepdims=True) + EPS)
    return (y * g.astype(jnp.float32)).astype(x.dtype)


def modulate(h, shift, scale):
    return h * (1.0 + scale) + shift


def flip(a):
    return jnp.flip(a, axis=1)


def centred_dwconv(x, w, b):
    left = CONV_W // 2
    y = lax.conv_general_dilated(x, w.astype(x.dtype)[:, None, :], (1,), [(left, CONV_W - 1 - left)],
                                 dimension_numbers=('NWC', 'WIO', 'NWC'), feature_group_count=x.shape[-1])
    return y + b.astype(x.dtype)


def hgrn_gate(f_raw, lb):
    bsz, t_len, _ = f_raw.shape
    f = lb + (1.0 - lb) * jax.nn.sigmoid(f_raw.astype(jnp.float32))
    shp = (bsz, t_len, HG_HEADS, HG_DK)
    return (1.0 - f).reshape(shp), jnp.log(f).reshape(shp)


def hgrn_chunk_scan(q, k, v, log_f, s0):
    bsz, t_len, n_h, _ = q.shape
    n_chunks = t_len // HG_CHUNK

    def to_chunks(a):
        return a.reshape(bsz, n_chunks, HG_CHUNK, n_h, a.shape[-1]).transpose(1, 0, 3, 2, 4)

    lower = jnp.tril(jnp.ones((HG_CHUNK, HG_CHUNK), dtype=bool))[:, :, None]

    def step(state, inp):
        qc, kc, vc, gc = inp
        b = jnp.cumsum(gc, axis=2)
        o = jnp.einsum('bhtd,bhdv->bhtv', qc * jnp.exp(b), state)
        rel = jnp.where(lower, b[:, :, :, None, :] - b[:, :, None, :, :], -jnp.inf)
        att = jnp.einsum('bhtd,bhsd,bhtsd->bhts', qc, kc, jnp.exp(rel))
        o = o + jnp.einsum('bhts,bhsv->bhtv', att, vc)
        b_end = b[:, :, -1]
        state = jnp.exp(b_end)[..., None] * state + jnp.einsum(
            'bhsd,bhsv->bhdv', kc * jnp.exp(b_end[:, :, None] - b), vc)
        return state, o

    s_end, o = lax.scan(step, s0, (to_chunks(q), to_chunks(k), to_chunks(v), to_chunks(log_f)))
    return o.transpose(1, 0, 3, 2, 4).reshape(bsz, t_len, n_h, v.shape[-1]), s_end


def hgrn_bidir(q, v, k_f, lf_f, k_b, lf_b, s_f, s_b):
    o_f, s_f = hgrn_chunk_scan(q, k_f, v, lf_f, s_f)
    o_b, s_b = hgrn_chunk_scan(flip(q), flip(k_b), flip(v), flip(lf_b), s_b)
    return o_f + flip(o_b), s_f, s_b


def block_diag(x, w):
    bsz, t_len, _ = x.shape
    y = jnp.einsum('btnc,ncd->btnd', x.reshape(bsz, t_len, RG_BLOCKS, RG_BW), w)
    return y.reshape(bsz, t_len, RG_W)


def rglru_gates(xc, wa, ba, wx, bx, lam):
    r = jax.nn.sigmoid(block_diag(xc, wa) + ba)
    i = jax.nn.sigmoid(block_diag(xc, wx) + bx)
    log_a = -RG_C * r * jax.nn.softplus(-lam.astype(jnp.float32))
    u = jnp.sqrt(-jnp.expm1(2.0 * log_a)) * (i * xc)
    return log_a, u


def linear_scan(log_a, u, h0):
    a = jnp.exp(log_a)
    u = u.at[:, 0].add(a[:, 0] * h0)

    def combine(l, r):
        return l[0] * r[0], r[0] * l[1] + r[1]

    _, h = lax.associative_scan(combine, (a, u), axis=1)
    return h, h[:, -1]


def rglru_bidir(g_f, g_b, h_f, h_b):
    y_f, h_f = linear_scan(g_f[0], g_f[1], h_f)
    y_b, h_b = linear_scan(flip(g_b[0]), flip(g_b[1]), h_b)
    return y_f + flip(y_b), h_f, h_b


def even_mixer(hl, hc, w_in, w_out, lb, hg_gain, conv_w, conv_b, wa, ba, wx, bx, lam):
    def prepare(h):
        bsz, t_len, _ = h.shape
        q, i, f_fw, f_bw, g, xr, gate = jnp.split(pmm3(h, w_in), EVEN_SPLITS, axis=-1)
        q = jax.nn.silu(q.astype(jnp.float32)).reshape(bsz, t_len, HG_HEADS, HG_DK)
        v = i.astype(jnp.float32).reshape(bsz, t_len, HG_HEADS, HG_DV)
        k_f, lf_f = hgrn_gate(f_fw, lb[0])
        k_b, lf_b = hgrn_gate(f_bw, lb[1])
        xc = centred_dwconv(xr, conv_w, conv_b).astype(jnp.float32)
        g_f = rglru_gates(xc, wa[0], ba[0], wx[0], bx[0], lam[0])
        g_b = rglru_gates(xc, wa[1], ba[1], wx[1], bx[1], lam[1])
        return (q, v, k_f, lf_f, k_b, lf_b), (g_f, g_b), (g, gate)

    def merge(o_hg, h_rg, g, gate):
        bsz, t_len, _ = g.shape
        y_a = rms_norm(o_hg, hg_gain) * jax.nn.silu(g.astype(jnp.float32)).reshape(bsz, t_len, HG_HEADS, HG_DV)
        y_b = jax.nn.gelu(gate.astype(jnp.float32)) * h_rg
        y = jnp.concatenate([y_a.reshape(bsz, t_len, HG_WV), y_b], axis=-1)
        return pmm3(y.astype(g.dtype), w_out)

    hg_c, rg_c, gt_c = prepare(hc)
    hg_l, rg_l, gt_l = prepare(hl)
    bsz = hl.shape[0]
    s0 = jnp.zeros((bsz, HG_HEADS, HG_DK, HG_DV), jnp.float32)
    h0 = jnp.zeros((bsz, RG_W), jnp.float32)
    o_c, s_f, s_b = hgrn_bidir(*hg_c, s0, s0)
    r_c, h_f, h_b = rglru_bidir(*rg_c, h0, h0)
    o_l, _, _ = hgrn_bidir(*hg_l, s_f, s_b)
    r_l, _, _ = rglru_bidir(*rg_l, h_f, h_b)
    return merge(o_l, r_l, *gt_l), merge(o_c, r_c, *gt_c)


def axial_rope(t_len):
    rows = t_len // GRID_W
    row = jnp.repeat(jnp.arange(rows, dtype=jnp.float32), GRID_W)
    col = jnp.tile(jnp.arange(GRID_W, dtype=jnp.float32), rows)
    n_freq = ATT_DH // 4
    inv = ROPE_BASE ** (-jnp.arange(n_freq, dtype=jnp.float32) / n_freq)
    ang = jnp.concatenate([row[:, None] * inv, col[:, None] * inv], axis=-1)
    return jnp.cos(ang), jnp.sin(ang)


def apply_rope(x, cos, sin):
    xf = x.astype(jnp.float32)
    x1, x2 = jnp.split(xf, 2, axis=-1)
    cs, sn = cos[None, :, None, :], sin[None, :, None, :]
    return jnp.concatenate([x1 * cs - x2 * sn, x2 * cs + x1 * sn], axis=-1).astype(x.dtype)


def sink_attention(q, k, v, sink, mask):
    s = jnp.einsum('bqhgd,bkhd->bhgqk', q, k).astype(jnp.float32) * (ATT_DH ** -0.5)
    if mask is not None:
        s = jnp.where(mask, s, -jnp.inf)
    sk = jnp.broadcast_to(sink.astype(jnp.float32)[None, :, :, None, None], s.shape[:-1] + (1,))
    p = jax.nn.softmax(jnp.concatenate([sk, s], axis=-1), axis=-1)[..., 1:]
    return jnp.einsum('bhgqk,bkhd->bqhgd', p.astype(v.dtype), v)


def windowed_attention(q, k, v, kc, vc, sink):
    bsz, t_len = q.shape[:2]
    n_blk = t_len // ATT_BLOCK
    n_ctx = kc.shape[1]
    pad = ((0, 0), (ATT_BLOCK, ATT_BLOCK), (0, 0), (0, 0))
    kp, vp = jnp.pad(k, pad), jnp.pad(v, pad)
    qb = q.reshape(bsz, n_blk, ATT_BLOCK, ATT_HKV, ATT_G, ATT_DH).transpose(1, 0, 2, 3, 4, 5)
    offs_q = jnp.arange(ATT_BLOCK)
    offs_k = jnp.arange(3 * ATT_BLOCK)
    ctx_mask = jnp.ones((ATT_BLOCK, n_ctx), dtype=bool)

    def block(args):
        n, qn = args
        kn = lax.dynamic_slice_in_dim(kp, n * ATT_BLOCK, 3 * ATT_BLOCK, axis=1)
        vn = lax.dynamic_slice_in_dim(vp, n * ATT_BLOCK, 3 * ATT_BLOCK, axis=1)
        qpos = n * ATT_BLOCK + offs_q
        kpos = (n - 1) * ATT_BLOCK + offs_k
        local = (jnp.abs(qpos[:, None] - kpos[None, :]) <= WINDOW) & (kpos >= 0)[None, :] & (kpos < t_len)[None, :]
        mask = jnp.concatenate([ctx_mask, local], axis=1)
        return sink_attention(qn, jnp.concatenate([kc, kn], axis=1), jnp.concatenate([vc, vn], axis=1), sink, mask)

    o = lax.map(block, (jnp.arange(n_blk), qb))
    return o.transpose(1, 0, 2, 3, 4, 5).reshape(bsz, t_len, ATT_WQ)


def fourier_mix(z):
    bsz, t_len, _ = z.shape
    zz = z.astype(jnp.float32).reshape(bsz, t_len, FN_GROUPS, FN_DIM)
    y = jnp.fft.fft2(zz, axes=(1, 3), norm='ortho').real
    return y.reshape(bsz, t_len, FN_W).astype(z.dtype)


def odd_mixer(hl, hc, w_in, w_out, q_gain, k_gain, sink, need_ctx):
    bsz, t_len, _ = hl.shape
    n_ctx = hc.shape[1]
    ql, kl, vl, fl = jnp.split(pmm3(hl, w_in), ODD_SPLITS, axis=-1)
    cos, sin = axial_rope(t_len)
    ql = apply_rope(rms_norm(ql.reshape(bsz, t_len, ATT_HQ, ATT_DH), q_gain), cos, sin)
    kl = apply_rope(rms_norm(kl.reshape(bsz, t_len, ATT_HKV, ATT_DH), k_gain), cos, sin)
    vl = vl.reshape(bsz, t_len, ATT_HKV, ATT_DH)
    kc, vc = jnp.split(pmm3(hc, w_in[:, ATT_WQ:ATT_WQ + 2 * ATT_WKV]), 2, axis=-1)
    kc = rms_norm(kc.reshape(bsz, n_ctx, ATT_HKV, ATT_DH), k_gain)
    vc = vc.reshape(bsz, n_ctx, ATT_HKV, ATT_DH)
    sink_g = sink.reshape(ATT_HKV, ATT_G)
    a_l = windowed_attention(ql.reshape(bsz, t_len, ATT_HKV, ATT_G, ATT_DH), kl, vl, kc, vc, sink_g)
    y_l = pmm3(jnp.concatenate([a_l, fourier_mix(fl)], axis=-1), w_out)
    if not need_ctx:
        return y_l, None
    qc = rms_norm(pmm3(hc, w_in[:, :ATT_WQ]).reshape(bsz, n_ctx, ATT_HKV, ATT_G, ATT_DH), q_gain)
    fc = pmm3(hc, w_in[:, ATT_WQ + 2 * ATT_WKV:])
    a_c = sink_attention(qc, kc, vc, sink_g, None).reshape(bsz, n_ctx, ATT_WQ)
    y_c = pmm3(jnp.concatenate([a_c, fourier_mix(fc)], axis=-1), w_out)
    return y_l, y_c


NT_DIMS = (((1,), (1,)), ((), ()))
PEER_TA = 256
PEER_TG = 8
PEER_GPB = 2
PEER_ROWS = PEER_TG * PK_HEADS * PK_TOPK
PEER_CHUNK = 4096
SC_W = 32
SC_IDX_BLOCK = 128


def _topk_rows(s, k):
    r, t = s.shape
    iota = lax.broadcasted_iota(jnp.int32, (r, t), 0)
    kiota = lax.broadcasted_iota(jnp.int32, (k, t), 0)
    vals = jnp.zeros((k, t), jnp.float32)
    rows = jnp.zeros((k, t), jnp.int32)
    for j in range(k):
        m = jnp.max(s, axis=0, keepdims=True)
        am = jnp.min(jnp.where(s == m, iota, r), axis=0, keepdims=True)
        vals = jnp.where(kiota == j, m, vals)
        rows = jnp.where(kiota == j, am, rows)
        s = jnp.where(iota == am, -jnp.inf, s)
    return vals, rows


def _route_kernel(x_ref, wq_ref, bq_ref, keys_ref, eidx_ref, gate_ref):
    xb = x_ref[...].astype(jnp.bfloat16)

    def head(h, carry):
        q = jnp.dot(xb, wq_ref[h], preferred_element_type=jnp.float32) + bq_ref[h]
        sv, si = [], []
        for p in range(2):
            qs = q[:, p * PK_DKH:(p + 1) * PK_DKH].astype(jnp.bfloat16)
            s_t = lax.dot_general(keys_ref[h, p], qs, NT_DIMS, preferred_element_type=jnp.float32)
            v, i = _topk_rows(s_t, PK_TOPK)
            sv.append(v)
            si.append(i)
        cand = [sv[0][0:1] + sv[1]]
        cidx = [si[0][0:1] * PK_NKEYS + si[1]]
        for a in range(1, 8):
            cand.append(sv[0][a:a + 1] + sv[1][0:8])
            cidx.append(si[0][a:a + 1] * PK_NKEYS + si[1][0:8])
        cand.append(sv[0][8:16] + sv[1][0:1])
        cidx.append(si[0][8:16] * PK_NKEYS + si[1][0:1])
        cand = jnp.concatenate(cand, axis=0)
        cidx = jnp.concatenate(cidx, axis=0)
        best, pos = _topk_rows(cand, PK_TOPK)
        r, t = cand.shape
        iota = lax.broadcasted_iota(jnp.int32, (r, t), 0)
        kiota = lax.broadcasted_iota(jnp.int32, (PK_TOPK, t), 0)
        eidx = jnp.zeros((PK_TOPK, t), jnp.int32)
        for j in range(PK_TOPK):
            e = jnp.max(jnp.where(iota == pos[j:j + 1], cidx, -1), axis=0, keepdims=True)
            eidx = jnp.where(kiota == j, e, eidx)
        ex = jnp.exp(best - best[0:1])
        eidx_ref[h] = eidx
        gate_ref[h] = ex / jnp.sum(ex, axis=0, keepdims=True)
        return carry

    lax.fori_loop(0, PK_HEADS, head, 0)


def peer_route(x, wq3, bq3, keys_b):
    n, d = x.shape
    return pl.pallas_call(
        _route_kernel,
        out_shape=(jax.ShapeDtypeStruct((PK_HEADS, PK_TOPK, n), jnp.int32),
                   jax.ShapeDtypeStruct((PK_HEADS, PK_TOPK, n), jnp.float32)),
        grid=(n // PEER_TA,),
        in_specs=[pl.BlockSpec((PEER_TA, d), lambda i: (i, 0)),
                  pl.BlockSpec((PK_HEADS, d, PK_DK), lambda i: (0, 0, 0)),
                  pl.BlockSpec((PK_HEADS, 1, PK_DK), lambda i: (0, 0, 0)),
                  pl.BlockSpec((PK_HEADS, 2, PK_NKEYS, PK_DKH), lambda i: (0, 0, 0, 0))],
        out_specs=(pl.BlockSpec((PK_HEADS, PK_TOPK, PEER_TA), lambda i: (0, 0, i)),
                   pl.BlockSpec((PK_HEADS, PK_TOPK, PEER_TA), lambda i: (0, 0, i))),
        compiler_params=pltpu.CompilerParams(dimension_semantics=("parallel",)),
        name="peer_route",
    )(x, wq3, bq3, keys_b)


def _mix_kernel(x_ref, gate_ref, g_ref, o_ref):
    col_tok = lax.broadcasted_iota(jnp.int32, (PEER_TG, PEER_ROWS), 1) // (PK_HEADS * PK_TOPK)
    own = col_tok == lax.broadcasted_iota(jnp.int32, (PEER_TG, PEER_ROWS), 0)
    for g in range(PEER_GPB):
        xb = x_ref[pl.ds(g * PEER_TG, PEER_TG), :].astype(jnp.bfloat16)
        acc = jnp.zeros((PEER_TG, PEER_ROWS), jnp.float32)
        for s in range(8):
            w = g_ref[pl.ds(g * PEER_ROWS, PEER_ROWS), s, :]
            ub = lax.bitcast_convert_type(w << 16, jnp.float32).astype(jnp.bfloat16)
            acc = acc + lax.dot_general(xb[:, s * 128:(s + 1) * 128], ub, NT_DIMS,
                                        preferred_element_type=jnp.float32)
        hsel = jnp.sum(jnp.where(own, acc, 0.0), axis=0, keepdims=True)
        act = jax.nn.gelu(hsel) * gate_ref[g]
        a = jnp.where(own, act, 0.0).astype(jnp.bfloat16)
        for s in range(8):
            w = g_ref[pl.ds(g * PEER_ROWS, PEER_ROWS), s, :]
            vb = lax.bitcast_convert_type(w & jnp.int32(-65536), jnp.float32).astype(jnp.bfloat16)
            o_ref[pl.ds(g * PEER_TG, PEER_TG), s * 128:(s + 1) * 128] = jnp.dot(
                a, vb, preferred_element_type=jnp.float32)


def peer_mix(x, gate_g, gathered):
    n, d = x.shape
    tb = PEER_TG * PEER_GPB
    return pl.pallas_call(
        _mix_kernel,
        out_shape=jax.ShapeDtypeStruct((n, d), jnp.float32),
        grid=(n // tb,),
        in_specs=[pl.BlockSpec((tb, d), lambda i: (i, 0)),
                  pl.BlockSpec((PEER_GPB, 1, PEER_ROWS), lambda i: (i, 0, 0)),
                  pl.BlockSpec((PEER_GPB * PEER_ROWS, 8, 128), lambda i: (i, 0, 0))],
        out_specs=pl.BlockSpec((tb, d), lambda i: (i, 0)),
        compiler_params=pltpu.CompilerParams(dimension_semantics=("parallel",),
                                             vmem_limit_bytes=48 * 1024 * 1024),
        name="peer_mix",
    )(x, gate_g, gathered)


def sc_gather(table, idxp):
    nblk = idxp.shape[1] // SC_IDX_BLOCK
    mesh = plsc.VectorSubcoreMesh(core_axis_name="c", subcore_axis_name="s")

    @functools.partial(pl.kernel, out_type=jax.ShapeDtypeStruct((nblk * SC_W, 8, 128), table.dtype), mesh=mesh)
    def gather_kernel(t_hbm, i_hbm, o_hbm):
        def body(i_vmem, o_vmem):
            pltpu.sync_copy(t_hbm.at[i_vmem.at[0, pl.ds(0, SC_W)]], o_vmem)

        pltpu.emit_pipeline(
            body, grid=(nblk,),
            in_specs=[pl.BlockSpec((1, SC_IDX_BLOCK), lambda i: (0, i))],
            out_specs=[pl.BlockSpec((SC_W, 8, 128), lambda i: (i, 0, 0))],
            core_axis_name=("c", "s"),
            dimension_semantics=(pltpu.PARALLEL,),
        )(i_hbm, o_hbm)

    return gather_kernel(table, idxp)


def pack_table(u, v):
    ub = lax.bitcast_convert_type(u.astype(jnp.bfloat16), jnp.uint16).astype(jnp.uint32)
    vb = lax.bitcast_convert_type(v.astype(jnp.bfloat16), jnp.uint16).astype(jnp.uint32)
    w = lax.bitcast_convert_type(ub | (vb << 16), jnp.int32)
    return w.reshape(u.shape[0], 8, 128)


def peer(h, w_q, b_q, sub_keys, u, v):
    n, d = h.shape
    table = pack_table(u, v)
    wq3 = w_q.reshape(d, PK_HEADS, PK_DK).transpose(1, 0, 2).astype(jnp.bfloat16)
    bq3 = b_q.reshape(PK_HEADS, 1, PK_DK)
    qq = (pmm(h, w_q) + b_q).reshape(n, PK_HEADS, 2, PK_DKH)
    ss = jnp.einsum('thpd,hpkd->thpk', qq, sub_keys.astype(jnp.float32))
    sv, si = lax.top_k(ss, PK_TOPK)
    cand = (sv[:, :, 0, :, None] + sv[:, :, 1, None, :]).reshape(n, PK_HEADS, PK_TOPK * PK_TOPK)
    cidx = (si[:, :, 0, :, None] * PK_NKEYS + si[:, :, 1, None, :]).reshape(n, PK_HEADS, PK_TOPK * PK_TOPK)
    best, pos = lax.top_k(cand, PK_TOPK)
    eidx = jnp.take_along_axis(cidx, pos, axis=-1).reshape(n, PK_HEADS * PK_TOPK)
    gate = jax.nn.softmax(best, axis=-1).reshape(n, PK_HEADS * PK_TOPK)
    outs = []
    for c0 in range(0, n, PEER_CHUNK):
        c1 = min(n, c0 + PEER_CHUNK)
        e = eidx[c0:c1].reshape(-1, SC_W)
        idxp = jnp.pad(e, ((0, 0), (0, SC_IDX_BLOCK - SC_W))).reshape(1, -1)
        gathered = sc_gather(table, idxp)
        gg = gate[c0:c1].reshape((c1 - c0) // PEER_TG, 1, PEER_ROWS)
        outs.append(peer_mix(h[c0:c1], gg, gathered))
    return jnp.concatenate(outs, axis=0)


def kernel(x, c, ctx, c_ctx, w_mod, b_mod, g_mix, g_ffn, e_w_in, e_w_out, hg_lb_logits, hg_gain, rg_conv_w,
           rg_conv_b, rg_wa, rg_ba, rg_wx, rg_bx, rg_lambda, o_w_in, o_w_out, q_gain, k_gain, sinks, p_wq,
           p_bq, p_keys, p_u, p_v):
    lb_all = jnp.cumsum(jax.nn.softmax(hg_lb_logits.astype(jnp.float32), axis=1), axis=1)
    lat, cx = x, ctx
    for l in range(DEPTH):
        last = l == DEPTH - 1
        j = l // 2
        m_lat = jnp.split((jax.nn.silu(c) @ w_mod[l] + b_mod[l])[:, None, :], N_MOD, axis=-1)
        m_ctx = jnp.split(jax.nn.silu(c_ctx) @ w_mod[l] + b_mod[l], N_MOD, axis=-1)
        hl = modulate(rms_norm(lat, g_mix[l]), m_lat[0], m_lat[1])
        hc = modulate(rms_norm(cx, g_mix[l]), m_ctx[0], m_ctx[1])
        if l % 2 == 0:
            yl, yc = even_mixer(hl, hc, e_w_in[j], e_w_out[j], lb_all[:, j], hg_gain[j], rg_conv_w[j], rg_conv_b[j],
                                rg_wa[j], rg_ba[j], rg_wx[j], rg_bx[j], rg_lambda[j])
        else:
            yl, yc = odd_mixer(hl, hc, o_w_in[j], o_w_out[j], q_gain[j], k_gain[j], sinks[j], not last)
        lat = lat + m_lat[2] * yl
        hl2 = modulate(rms_norm(lat, g_ffn[l]), m_lat[3], m_lat[4])
        n_lat = hl2.shape[0] * hl2.shape[1]
        if last:
            ff = peer(hl2.reshape(n_lat, hl2.shape[-1]), p_wq[l], p_bq[l], p_keys[l], p_u[l], p_v[l])
            lat = lat + m_lat[5] * ff.reshape(lat.shape)
        else:
            cx = cx + m_ctx[2] * yc
            hc2 = modulate(rms_norm(cx, g_ffn[l]), m_ctx[3], m_ctx[4])
            tokens = jnp.concatenate([hl2.reshape(n_lat, hl2.shape[-1]), hc2.reshape(-1, hc2.shape[-1])], axis=0)
            ff = peer(tokens, p_wq[l], p_bq[l], p_keys[l], p_u[l], p_v[l])
            lat = lat + m_lat[5] * ff[:n_lat].reshape(lat.shape)
            cx = cx + m_ctx[5] * ff[n_lat:].reshape(cx.shape)
    return lat
```

```python
import functools
import math

import jax
import jax.numpy as jnp
import numpy as np
from jax import lax
from jax.experimental import pallas as pl
from jax.experimental.pallas import tpu as pltpu
from jax.experimental.pallas import tpu_sc as plsc

D_MODEL = 1024
DEPTH = 2
GRID_W = 64
EPS = 1e-6
N_MOD = 6

HG_HEADS = 4
HG_DK = 128
HG_DV = 128
HG_WK = HG_HEADS * HG_DK
HG_WV = HG_HEADS * HG_DV
HG_CHUNK = 64
RG_W = 512
RG_BLOCKS = 8
RG_BW = RG_W // RG_BLOCKS
RG_C = 8.0
CONV_W = 4
EVEN_SPLITS = (HG_WK, HG_WK + HG_WV, 2 * HG_WK + HG_WV, 3 * HG_WK + HG_WV, 3 * HG_WK + 2 * HG_WV,
               3 * HG_WK + 2 * HG_WV + RG_W)

ATT_HQ = 12
ATT_HKV = 4
ATT_G = ATT_HQ // ATT_HKV
ATT_DH = 64
ATT_WQ = ATT_HQ * ATT_DH
ATT_WKV = ATT_HKV * ATT_DH
WINDOW = 128
ATT_BLOCK = 128
ROPE_BASE = 10000.0
FN_GROUPS = 4
FN_DIM = 64
FN_W = FN_GROUPS * FN_DIM
ODD_SPLITS = (ATT_WQ, ATT_WQ + ATT_WKV, ATT_WQ + 2 * ATT_WKV)

PK_HEADS = 8
PK_NKEYS = 128
PK_DK = 256
PK_DKH = PK_DK // 2
PK_TOPK = 16
PK_CHUNK = 128


def _mm_kernel(a_ref, b_ref, o_ref):
    o_ref[...] = jnp.dot(a_ref[...].astype(jnp.bfloat16), b_ref[...].astype(jnp.bfloat16),
                         preferred_element_type=jnp.float32)


def pmm(a, b, tm=512, tn=512):
    m, k = a.shape
    n = b.shape[1]
    tm = next(t for t in (tm, 256, 128, m) if m % t == 0)
    tn = next(t for t in (tn, 256, 128, n) if n % t == 0)
    return pl.pallas_call(
        _mm_kernel,
        out_shape=jax.ShapeDtypeStruct((m, n), jnp.float32),
        grid=(m // tm, n // tn),
        in_specs=[pl.BlockSpec((tm, k), lambda i, j: (i, 0)),
                  pl.BlockSpec((k, tn), lambda i, j: (0, j))],
        out_specs=pl.BlockSpec((tm, tn), lambda i, j: (i, j)),
        compiler_params=pltpu.CompilerParams(dimension_semantics=("parallel", "parallel")),
        name="pmm",
    )(a, b)


def pmm3(a, b):
    bsz, t, k = a.shape
    return pmm(a.reshape(bsz * t, k), b).reshape(bsz, t, b.shape[1])


def rms_norm(x, g):
    xf = x.astype(jnp.float32)
    y = xf * lax.rsqrt(jnp.mean(xf * xf, axis=-1, keepdims=True) + EPS)
    return (y * g.astype(jnp.float32)).astype(x.dtype)


def modulate(h, shift, scale):
    return h * (1.0 + scale) + shift


def flip(a):
    return jnp.flip(a, axis=1)


def centred_dwconv(x, w, b):
    left = CONV_W // 2
    y = lax.conv_general_dilated(x, w.astype(x.dtype)[:, None, :], (1,), [(left, CONV_W - 1 - left)],
                                 dimension_numbers=('NWC', 'WIO', 'NWC'), feature_group_count=x.shape[-1])
    return y + b.astype(x.dtype)


def hgrn_gate(f_raw, lb):
    bsz, t_len, _ = f_raw.shape
    f = lb + (1.0 - lb) * jax.nn.sigmoid(f_raw.astype(jnp.float32))
    shp = (bsz, t_len, HG_HEADS, HG_DK)
    return (1.0 - f).reshape(shp), jnp.log(f).reshape(shp)


def hgrn_chunk_scan(q, k, v, log_f, s0):
    bsz, t_len, n_h, _ = q.shape
    n_chunks = t_len // HG_CHUNK

    def to_chunks(a):
        return a.reshape(bsz, n_chunks, HG_CHUNK, n_h, a.shape[-1]).transpose(1, 0, 3, 2, 4)

    lower = jnp.tril(jnp.ones((HG_CHUNK, HG_CHUNK), dtype=bool))[:, :, None]

    def step(state, inp):
        qc, kc, vc, gc = inp
        b = jnp.cumsum(gc, axis=2)
        o = jnp.einsum('bhtd,bhdv->bhtv', qc * jnp.exp(b), state)
        rel = jnp.where(lower, b[:, :, :, None, :] - b[:, :, None, :, :], -jnp.inf)
        att = jnp.einsum('bhtd,bhsd,bhtsd->bhts', qc, kc, jnp.exp(rel))
        o = o + jnp.einsum('bhts,bhsv->bhtv', att, vc)
        b_end = b[:, :, -1]
        state = jnp.exp(b_end)[..., None] * state + jnp.einsum(
            'bhsd,bhsv->bhdv', kc * jnp.exp(b_end[:, :, None] - b), vc)
        return state, o

    s_end, o = lax.scan(step, s0, (to_chunks(q), to_chunks(k), to_chunks(v), to_chunks(log_f)))
    return o.transpose(1, 0, 3, 2, 4).reshape(bsz, t_len, n_h, v.shape[-1]), s_end


def hgrn_bidir(q, v, k_f, lf_f, k_b, lf_b, s_f, s_b):
    o_f, s_f = hgrn_chunk_scan(q, k_f, v, lf_f, s_f)
    o_b, s_b = hgrn_chunk_scan(flip(q), flip(k_b), flip(v), flip(lf_b), s_b)
    return o_f + flip(o_b), s_f, s_b


def block_diag(x, w):
    bsz, t_len, _ = x.shape
    y = jnp.einsum('btnc,ncd->btnd', x.reshape(bsz, t_len, RG_BLOCKS, RG_BW), w)
    return y.reshape(bsz, t_len, RG_W)


def rglru_gates(xc, wa, ba, wx, bx, lam):
    r = jax.nn.sigmoid(block_diag(xc, wa) + ba)
    i = jax.nn.sigmoid(block_diag(xc, wx) + bx)
    log_a = -RG_C * r * jax.nn.softplus(-lam.astype(jnp.float32))
    u = jnp.sqrt(-jnp.expm1(2.0 * log_a)) * (i * xc)
    return log_a, u


def linear_scan(log_a, u, h0):
    a = jnp.exp(log_a)
    u = u.at[:, 0].add(a[:, 0] * h0)

    def combine(l, r):
        return l[0] * r[0], r[0] * l[1] + r[1]

    _, h = lax.associative_scan(combine, (a, u), axis=1)
    return h, h[:, -1]


def rglru_bidir(g_f, g_b, h_f, h_b):
    y_f, h_f = linear_scan(g_f[0], g_f[1], h_f)
    y_b, h_b = linear_scan(flip(g_b[0]), flip(g_b[1]), h_b)
    return y_f + flip(y_b), h_f, h_b


def even_mixer(hl, hc, w_in, w_out, lb, hg_gain, conv_w, conv_b, wa, ba, wx, bx, lam):
    def prepare(h):
        bsz, t_len, _ = h.shape
        q, i, f_fw, f_bw, g, xr, gate = jnp.split(pmm3(h, w_in), EVEN_SPLITS, axis=-1)
        q = jax.nn.silu(q.astype(jnp.float32)).reshape(bsz, t_len, HG_HEADS, HG_DK)
        v = i.astype(jnp.float32).reshape(bsz, t_len, HG_HEADS, HG_DV)
        k_f, lf_f = hgrn_gate(f_fw, lb[0])
        k_b, lf_b = hgrn_gate(f_bw, lb[1])
        xc = centred_dwconv(xr, conv_w, conv_b).astype(jnp.float32)
        g_f = rglru_gates(xc, wa[0], ba[0], wx[0], bx[0], lam[0])
        g_b = rglru_gates(xc, wa[1], ba[1], wx[1], bx[1], lam[1])
        return (q, v, k_f, lf_f, k_b, lf_b), (g_f, g_b), (g, gate)

    def merge(o_hg, h_rg, g, gate):
        bsz, t_len, _ = g.shape
        y_a = rms_norm(o_hg, hg_gain) * jax.nn.silu(g.astype(jnp.float32)).reshape(bsz, t_len, HG_HEADS, HG_DV)
        y_b = jax.nn.gelu(gate.astype(jnp.float32)) * h_rg
        y = jnp.concatenate([y_a.reshape(bsz, t_len, HG_WV), y_b], axis=-1)
        return pmm3(y.astype(g.dtype), w_out)

    hg_c, rg_c, gt_c = prepare(hc)
    hg_l, rg_l, gt_l = prepare(hl)
    bsz = hl.shape[0]
    s0 = jnp.zeros((bsz, HG_HEADS, HG_DK, HG_DV), jnp.float32)
    h0 = jnp.zeros((bsz, RG_W), jnp.float32)
    o_c, s_f, s_b = hgrn_bidir(*hg_c, s0, s0)
    r_c, h_f, h_b = rglru_bidir(*rg_c, h0, h0)
    o_l, _, _ = hgrn_bidir(*hg_l, s_f, s_b)
    r_l, _, _ = rglru_bidir(*rg_l, h_f, h_b)
    return merge(o_l, r_l, *gt_l), merge(o_c, r_c, *gt_c)


def axial_rope(t_len):
    rows = t_len // GRID_W
    row = jnp.repeat(jnp.arange(rows, dtype=jnp.float32), GRID_W)
    col = jnp.tile(jnp.arange(GRID_W, dtype=jnp.float32), rows)
    n_freq = ATT_DH // 4
    inv = ROPE_BASE ** (-jnp.arange(n_freq, dtype=jnp.float32) / n_freq)
    ang = jnp.concatenate([row[:, None] * inv, col[:, None] * inv], axis=-1)
    return jnp.cos(ang), jnp.sin(ang)


def apply_rope(x, cos, sin):
    xf = x.astype(jnp.float32)
    x1, x2 = jnp.split(xf, 2, axis=-1)
    cs, sn = cos[None, :, None, :], sin[None, :, None, :]
    return jnp.concatenate([x1 * cs - x2 * sn, x2 * cs + x1 * sn], axis=-1).astype(x.dtype)


def sink_attention(q, k, v, sink, mask):
    s = jnp.einsum('bqhgd,bkhd->bhgqk', q, k).astype(jnp.float32) * (ATT_DH ** -0.5)
    if mask is not None:
        s = jnp.where(mask, s, -jnp.inf)
    sk = jnp.broadcast_to(sink.astype(jnp.float32)[None, :, :, None, None], s.shape[:-1] + (1,))
    p = jax.nn.softmax(jnp.concatenate([sk, s], axis=-1), axis=-1)[..., 1:]
    return jnp.einsum('bhgqk,bkhd->bqhgd', p.astype(v.dtype), v)


def windowed_attention(q, k, v, kc, vc, sink):
    bsz, t_len = q.shape[:2]
    n_blk = t_len // ATT_BLOCK
    n_ctx = kc.shape[1]
    pad = ((0, 0), (ATT_BLOCK, ATT_BLOCK), (0, 0), (0, 0))
    kp, vp = jnp.pad(k, pad), jnp.pad(v, pad)
    qb = q.reshape(bsz, n_blk, ATT_BLOCK, ATT_HKV, ATT_G, ATT_DH).transpose(1, 0, 2, 3, 4, 5)
    offs_q = jnp.arange(ATT_BLOCK)
    offs_k = jnp.arange(3 * ATT_BLOCK)
    ctx_mask = jnp.ones((ATT_BLOCK, n_ctx), dtype=bool)

    def block(args):
        n, qn = args
        kn = lax.dynamic_slice_in_dim(kp, n * ATT_BLOCK, 3 * ATT_BLOCK, axis=1)
        vn = lax.dynamic_slice_in_dim(vp, n * ATT_BLOCK, 3 * ATT_BLOCK, axis=1)
        qpos = n * ATT_BLOCK + offs_q
        kpos = (n - 1) * ATT_BLOCK + offs_k
        local = (jnp.abs(qpos[:, None] - kpos[None, :]) <= WINDOW) & (kpos >= 0)[None, :] & (kpos < t_len)[None, :]
        mask = jnp.concatenate([ctx_mask, local], axis=1)
        return sink_attention(qn, jnp.concatenate([kc, kn], axis=1), jnp.concatenate([vc, vn], axis=1), sink, mask)

    o = lax.map(block, (jnp.arange(n_blk), qb))
    return o.transpose(1, 0, 2, 3, 4, 5).reshape(bsz, t_len, ATT_WQ)


def fourier_mix(z):
    bsz, t_len, _ = z.shape
    zz = z.astype(jnp.float32).reshape(bsz, t_len, FN_GROUPS, FN_DIM)
    y = jnp.fft.fft2(zz, axes=(1, 3), norm='ortho').real
    return y.reshape(bsz, t_len, FN_W).astype(z.dtype)


def odd_mixer(hl, hc, w_in, w_out, q_gain, k_gain, sink, need_ctx):
    bsz, t_len, _ = hl.shape
    n_ctx = hc.shape[1]
    ql, kl, vl, fl = jnp.split(pmm3(hl, w_in), ODD_SPLITS, axis=-1)
    cos, sin = axial_rope(t_len)
    ql = apply_rope(rms_norm(ql.reshape(bsz, t_len, ATT_HQ, ATT_DH), q_gain), cos, sin)
    kl = apply_rope(rms_norm(kl.reshape(bsz, t_len, ATT_HKV, ATT_DH), k_gain), cos, sin)
    vl = vl.reshape(bsz, t_len, ATT_HKV, ATT_DH)
    kc, vc = jnp.split(pmm3(hc, w_in[:, ATT_WQ:ATT_WQ + 2 * ATT_WKV]), 2, axis=-1)
    kc = rms_norm(kc.reshape(bsz, n_ctx, ATT_HKV, ATT_DH), k_gain)
    vc = vc.reshape(bsz, n_ctx, ATT_HKV, ATT_DH)
    sink_g = sink.reshape(ATT_HKV, ATT_G)
    a_l = windowed_attention(ql.reshape(bsz, t_len, ATT_HKV, ATT_G, ATT_DH), kl, vl, kc, vc, sink_g)
    y_l = pmm3(jnp.concatenate([a_l, fourier_mix(fl)], axis=-1), w_out)
    if not need_ctx:
        return y_l, None
    qc = rms_norm(pmm3(hc, w_in[:, :ATT_WQ]).reshape(bsz, n_ctx, ATT_HKV, ATT_G, ATT_DH), q_gain)
    fc = pmm3(hc, w_in[:, ATT_WQ + 2 * ATT_WKV:])
    a_c = sink_attention(qc, kc, vc, sink_g, None).reshape(bsz, n_ctx, ATT_WQ)
    y_c = pmm3(jnp.concatenate([a_c, fourier_mix(fc)], axis=-1), w_out)
    return y_l, y_c


NT_DIMS = (((1,), (1,)), ((), ()))
PEER_TA = 256
PEER_TG = 8
PEER_GPB = 2
PEER_SUB = 8
PEER_COLS = PK_HEADS * PK_TOPK * PEER_SUB
PEER_CHUNK = 4096
SC_W = 32
SC_IDX_BLOCK = 128


def _topk_rows(s, k):
    r, t = s.shape
    iota = lax.broadcasted_iota(jnp.int32, (r, t), 0)
    kiota = lax.broadcasted_iota(jnp.int32, (k, t), 0)
    vals = jnp.zeros((k, t), jnp.float32)
    rows = jnp.zeros((k, t), jnp.int32)
    for j in range(k):
        m = jnp.max(s, axis=0, keepdims=True)
        am = jnp.min(jnp.where(s == m, iota, r), axis=0, keepdims=True)
        vals = jnp.where(kiota == j, m, vals)
        rows = jnp.where(kiota == j, am, rows)
        s = jnp.where(iota == am, -jnp.inf, s)
    return vals, rows


def _route_kernel(x_ref, wq_ref, bq_ref, keys_ref, eidx_ref, gate_ref):
    xb = x_ref[...].astype(jnp.bfloat16)

    def head(h, carry):
        q = jnp.dot(xb, wq_ref[h], preferred_element_type=jnp.float32) + bq_ref[h]
        sv, si = [], []
        for p in range(2):
            qs = q[:, p * PK_DKH:(p + 1) * PK_DKH].astype(jnp.bfloat16)
            s_t = lax.dot_general(keys_ref[h, p], qs, NT_DIMS, preferred_element_type=jnp.float32)
            v, i = _topk_rows(s_t, PK_TOPK)
            sv.append(v)
            si.append(i)
        cand = [sv[0][0:1] + sv[1]]
        cidx = [si[0][0:1] * PK_NKEYS + si[1]]
        for a in range(1, 8):
            cand.append(sv[0][a:a + 1] + sv[1][0:8])
            cidx.append(si[0][a:a + 1] * PK_NKEYS + si[1][0:8])
        cand.append(sv[0][8:16] + sv[1][0:1])
        cidx.append(si[0][8:16] * PK_NKEYS + si[1][0:1])
        cand = jnp.concatenate(cand, axis=0)
        cidx = jnp.concatenate(cidx, axis=0)
        best, pos = _topk_rows(cand, PK_TOPK)
        r, t = cand.shape
        iota = lax.broadcasted_iota(jnp.int32, (r, t), 0)
        kiota = lax.broadcasted_iota(jnp.int32, (PK_TOPK, t), 0)
        eidx = jnp.zeros((PK_TOPK, t), jnp.int32)
        for j in range(PK_TOPK):
            e = jnp.max(jnp.where(iota == pos[j:j + 1], cidx, -1), axis=0, keepdims=True)
            eidx = jnp.where(kiota == j, e, eidx)
        ex = jnp.exp(best - best[0:1])
        eidx_ref[h] = eidx
        gate_ref[h] = ex / jnp.sum(ex, axis=0, keepdims=True)
        return carry

    lax.fori_loop(0, PK_HEADS, head, 0)


def peer_route(x, wq3, bq3, keys_b):
    n, d = x.shape
    return pl.pallas_call(
        _route_kernel,
        out_shape=(jax.ShapeDtypeStruct((PK_HEADS, PK_TOPK, n), jnp.int32),
                   jax.ShapeDtypeStruct((PK_HEADS, PK_TOPK, n), jnp.float32)),
        grid=(n // PEER_TA,),
        in_specs=[pl.BlockSpec((PEER_TA, d), lambda i: (i, 0)),
                  pl.BlockSpec((PK_HEADS, d, PK_DK), lambda i: (0, 0, 0)),
                  pl.BlockSpec((PK_HEADS, 1, PK_DK), lambda i: (0, 0, 0)),
                  pl.BlockSpec((PK_HEADS, 2, PK_NKEYS, PK_DKH), lambda i: (0, 0, 0, 0))],
        out_specs=(pl.BlockSpec((PK_HEADS, PK_TOPK, PEER_TA), lambda i: (0, 0, i)),
                   pl.BlockSpec((PK_HEADS, PK_TOPK, PEER_TA), lambda i: (0, 0, i))),
        compiler_params=pltpu.CompilerParams(dimension_semantics=("parallel",)),
        name="peer_route",
    )(x, wq3, bq3, keys_b)


def _lane_group_allsum(y, lane):
    k = 1
    while k < PEER_SUB:
        up = pltpu.roll(y, 128 - k, axis=1)
        dn = pltpu.roll(y, k, axis=1)
        y = y + jnp.where((lane & k) == 0, up, dn)
        k *= 2
    return y


def _mix_kernel(x2_ref, gate_ref, g_ref, o2_ref, hp_ref):
    lane = lax.broadcasted_iota(jnp.int32, (PEER_SUB, PEER_COLS), 1)
    sub = lax.broadcasted_iota(jnp.int32, (PEER_SUB, PEER_COLS), 0)
    same_s = (lane % PEER_SUB) == sub
    lane128 = lax.broadcasted_iota(jnp.int32, (PEER_TG, 128), 1)
    for g in range(PEER_GPB):
        for t in range(PEER_TG):
            tok = g * PEER_TG + t
            w = g_ref[pl.ds(tok * PEER_COLS, PEER_COLS), :]
            ub = lax.bitcast_convert_type(w << 16, jnp.float32).astype(jnp.bfloat16)
            x2 = x2_ref[pl.ds(tok * PEER_SUB, PEER_SUB), :].astype(jnp.bfloat16)
            p = lax.dot_general(x2, ub, NT_DIMS, preferred_element_type=jnp.float32)
            hp_ref[pl.ds(t, 1), :] = jnp.sum(jnp.where(same_s, p, 0.0), axis=0, keepdims=True)
        hp = hp_ref[...]
        hfull = jnp.concatenate(
            [_lane_group_allsum(hp[:, c * 128:(c + 1) * 128], lane128) for c in range(PEER_COLS // 128)], axis=1)
        act = jax.nn.gelu(hfull) * gate_ref[pl.ds(g * PEER_TG, PEER_TG), :]
        for t in range(PEER_TG):
            tok = g * PEER_TG + t
            a2 = jnp.where(same_s, act[t:t + 1, :], 0.0).astype(jnp.bfloat16)
            w = g_ref[pl.ds(tok * PEER_COLS, PEER_COLS), :]
            vb = lax.bitcast_convert_type(w & jnp.int32(-65536), jnp.float32).astype(jnp.bfloat16)
            o2_ref[pl.ds(tok * PEER_SUB, PEER_SUB), :] = jnp.dot(a2, vb, preferred_element_type=jnp.float32)


def peer_mix(x, gate, gathered):
    n, d = x.shape
    tb = PEER_TG * PEER_GPB
    x2 = x.reshape(n * PEER_SUB, 128)
    gate8 = jnp.repeat(gate, PEER_SUB, axis=1)
    g2 = gathered.reshape(n * PEER_COLS, 128)
    out2 = pl.pallas_call(
        _mix_kernel,
        out_shape=jax.ShapeDtypeStruct((n * PEER_SUB, 128), jnp.float32),
        grid=(n // tb,),
        in_specs=[pl.BlockSpec((tb * PEER_SUB, 128), lambda i: (i, 0)),
                  pl.BlockSpec((tb, PEER_COLS), lambda i: (i, 0)),
                  pl.BlockSpec((tb * PEER_COLS, 128), lambda i: (i, 0))],
        out_specs=pl.BlockSpec((tb * PEER_SUB, 128), lambda i: (i, 0)),
        scratch_shapes=[pltpu.VMEM((PEER_TG, PEER_COLS), jnp.float32)],
        compiler_params=pltpu.CompilerParams(dimension_semantics=("parallel",),
                                             vmem_limit_bytes=48 * 1024 * 1024),
        name="peer_mix",
    )(x2, gate8, g2)
    return out2.reshape(n, d)


def sc_gather(table, idxp):
    nblk = idxp.shape[1] // SC_IDX_BLOCK
    mesh = plsc.VectorSubcoreMesh(core_axis_name="c", subcore_axis_name="s")

    @functools.partial(pl.kernel, out_type=jax.ShapeDtypeStruct((nblk * SC_W, 8, 128), table.dtype), mesh=mesh)
    def gather_kernel(t_hbm, i_hbm, o_hbm):
        def body(i_vmem, o_vmem):
            pltpu.sync_copy(t_hbm.at[i_vmem.at[0, pl.ds(0, SC_W)]], o_vmem)

        pltpu.emit_pipeline(
            body, grid=(nblk,),
            in_specs=[pl.BlockSpec((1, SC_IDX_BLOCK), lambda i: (0, i))],
            out_specs=[pl.BlockSpec((SC_W, 8, 128), lambda i: (i, 0, 0))],
            core_axis_name=("c", "s"),
            dimension_semantics=(pltpu.PARALLEL,),
            trace_scopes=False,
        )(i_hbm, o_hbm)

    return gather_kernel(table, idxp)


def pack_table(u, v):
    ub = lax.bitcast_convert_type(u.astype(jnp.bfloat16), jnp.uint16).astype(jnp.uint32)
    vb = lax.bitcast_convert_type(v.astype(jnp.bfloat16), jnp.uint16).astype(jnp.uint32)
    w = lax.bitcast_convert_type(ub | (vb << 16), jnp.int32)
    return w.reshape(u.shape[0], 8, 128)


def peer(h, w_q, b_q, sub_keys, u, v):
    n, d = h.shape
    table = pack_table(u, v)
    wq3 = w_q.reshape(d, PK_HEADS, PK_DK).transpose(1, 0, 2).astype(jnp.bfloat16)
    bq3 = b_q.reshape(PK_HEADS, 1, PK_DK)
    qq = (pmm(h, w_q) + b_q).reshape(n, PK_HEADS, 2, PK_DKH)
    ss = jnp.einsum('thpd,hpkd->thpk', qq, sub_keys.astype(jnp.float32))
    sv, si = lax.top_k(ss, PK_TOPK)
    cand = (sv[:, :, 0, :, None] + sv[:, :, 1, None, :]).reshape(n, PK_HEADS, PK_TOPK * PK_TOPK)
    cidx = (si[:, :, 0, :, None] * PK_NKEYS + si[:, :, 1, None, :]).reshape(n, PK_HEADS, PK_TOPK * PK_TOPK)
    best, pos = lax.top_k(cand, PK_TOPK)
    eidx = jnp.take_along_axis(cidx, pos, axis=-1).reshape(n, PK_HEADS * PK_TOPK)
    gate = jax.nn.softmax(best, axis=-1).reshape(n, PK_HEADS * PK_TOPK)
    outs = []
    for c0 in range(0, n, PEER_CHUNK):
        c1 = min(n, c0 + PEER_CHUNK)
        e = eidx[c0:c1].reshape(-1, SC_W)
        idxp = jnp.pad(e, ((0, 0), (0, SC_IDX_BLOCK - SC_W))).reshape(1, -1)
        gathered = sc_gather(table, idxp)
        outs.append(peer_mix(h[c0:c1], gate[c0:c1], gathered))
    return jnp.concatenate(outs, axis=0)


def kernel(x, c, ctx, c_ctx, w_mod, b_mod, g_mix, g_ffn, e_w_in, e_w_out, hg_lb_logits, hg_gain, rg_conv_w,
           rg_conv_b, rg_wa, rg_ba, rg_wx, rg_bx, rg_lambda, o_w_in, o_w_out, q_gain, k_gain, sinks, p_wq,
           p_bq, p_keys, p_u, p_v):
    lb_all = jnp.cumsum(jax.nn.softmax(hg_lb_logits.astype(jnp.float32), axis=1), axis=1)
    lat, cx = x, ctx
    for l in range(DEPTH):
        last = l == DEPTH - 1
        j = l // 2
        m_lat = jnp.split((jax.nn.silu(c) @ w_mod[l] + b_mod[l])[:, None, :], N_MOD, axis=-1)
        m_ctx = jnp.split(jax.nn.silu(c_ctx) @ w_mod[l] + b_mod[l], N_MOD, axis=-1)
        hl = modulate(rms_norm(lat, g_mix[l]), m_lat[0], m_lat[1])
        hc = modulate(rms_norm(cx, g_mix[l]), m_ctx[0], m_ctx[1])
        if l % 2 == 0:
            yl, yc = even_mixer(hl, hc, e_w_in[j], e_w_out[j], lb_all[:, j], hg_gain[j], rg_conv_w[j], rg_conv_b[j],
                                rg_wa[j], rg_ba[j], rg_wx[j], rg_bx[j], rg_lambda[j])
        else:
            yl, yc = odd_mixer(hl, hc, o_w_in[j], o_w_out[j], q_gain[j], k_gain[j], sinks[j], not last)
        lat = lat + m_lat[2] * yl
        hl2 = modulate(rms_norm(lat, g_ffn[l]), m_lat[3], m_lat[4])
        n_lat = hl2.shape[0] * hl2.shape[1]
        if last:
            ff = peer(hl2.reshape(n_lat, hl2.shape[-1]), p_wq[l], p_bq[l], p_keys[l], p_u[l], p_v[l])
            lat = lat + m_lat[5] * ff.reshape(lat.shape)
        else:
            cx = cx + m_ctx[2] * yc
            hc2 = modulate(rms_norm(cx, g_ffn[l]), m_ctx[3], m_ctx[4])
            tokens = jnp.concatenate([hl2.reshape(n_lat, hl2.shape[-1]), hc2.reshape(-1, hc2.shape[-1])], axis=0)
            ff = peer(tokens, p_wq[l], p_bq[l], p_keys[l], p_u[l], p_v[l])
            lat = lat + m_lat[5] * ff[:n_lat].reshape(lat.shape)
            cx = cx + m_ctx[5] * ff[n_lat:].reshape(cx.shape)
    return lat
```

```python
import functools
import math

import jax
import jax.numpy as jnp
import numpy as np
from jax import lax
from jax.experimental import pallas as pl
from jax.experimental.pallas import tpu as pltpu
from jax.experimental.pallas import tpu_sc as plsc

D_MODEL = 1024
DEPTH = 2
GRID_W = 64
EPS = 1e-6
N_MOD = 6

HG_HEADS = 4
HG_DK = 128
HG_DV = 128
HG_WK = HG_HEADS * HG_DK
HG_WV = HG_HEADS * HG_DV
HG_CHUNK = 64
RG_W = 512
RG_BLOCKS = 8
RG_BW = RG_W // RG_BLOCKS
RG_C = 8.0
CONV_W = 4
EVEN_SPLITS = (HG_WK, HG_WK + HG_WV, 2 * HG_WK + HG_WV, 3 * HG_WK + HG_WV, 3 * HG_WK + 2 * HG_WV,
               3 * HG_WK + 2 * HG_WV + RG_W)

ATT_HQ = 12
ATT_HKV = 4
ATT_G = ATT_HQ // ATT_HKV
ATT_DH = 64
ATT_WQ = ATT_HQ * ATT_DH
ATT_WKV = ATT_HKV * ATT_DH
WINDOW = 128
ATT_BLOCK = 128
ROPE_BASE = 10000.0
FN_GROUPS = 4
FN_DIM = 64
FN_W = FN_GROUPS * FN_DIM
ODD_SPLITS = (ATT_WQ, ATT_WQ + ATT_WKV, ATT_WQ + 2 * ATT_WKV)

PK_HEADS = 8
PK_NKEYS = 128
PK_DK = 256
PK_DKH = PK_DK // 2
PK_TOPK = 16
PK_CHUNK = 128


def _mm_kernel(a_ref, b_ref, o_ref):
    o_ref[...] = jnp.dot(a_ref[...].astype(jnp.bfloat16), b_ref[...].astype(jnp.bfloat16),
                         preferred_element_type=jnp.float32)


def pmm(a, b, tm=512, tn=512):
    m, k = a.shape
    n = b.shape[1]
    tm = next(t for t in (tm, 256, 128, m) if m % t == 0)
    tn = next(t for t in (tn, 256, 128, n) if n % t == 0)
    return pl.pallas_call(
        _mm_kernel,
        out_shape=jax.ShapeDtypeStruct((m, n), jnp.float32),
        grid=(m // tm, n // tn),
        in_specs=[pl.BlockSpec((tm, k), lambda i, j: (i, 0)),
                  pl.BlockSpec((k, tn), lambda i, j: (0, j))],
        out_specs=pl.BlockSpec((tm, tn), lambda i, j: (i, j)),
        compiler_params=pltpu.CompilerParams(dimension_semantics=("parallel", "parallel")),
        name="pmm",
    )(a, b)


def pmm3(a, b):
    bsz, t, k = a.shape
    return pmm(a.reshape(bsz * t, k), b).reshape(bsz, t, b.shape[1])


def rms_norm(x, g):
    xf = x.astype(jnp.float32)
    y = xf * lax.rsqrt(jnp.mean(xf * xf, axis=-1, keepdims=True) + EPS)
    return (y * g.astype(jnp.float32)).astype(x.dtype)


def modulate(h, shift, scale):
    return h * (1.0 + scale) + shift


def flip(a):
    return jnp.flip(a, axis=1)


def centred_dwconv(x, w, b):
    left = CONV_W // 2
    y = lax.conv_general_dilated(x, w.astype(x.dtype)[:, None, :], (1,), [(left, CONV_W - 1 - left)],
                                 dimension_numbers=('NWC', 'WIO', 'NWC'), feature_group_count=x.shape[-1])
    return y + b.astype(x.dtype)


def hgrn_gate(f_raw, lb):
    bsz, t_len, _ = f_raw.shape
    f = lb + (1.0 - lb) * jax.nn.sigmoid(f_raw.astype(jnp.float32))
    shp = (bsz, t_len, HG_HEADS, HG_DK)
    return (1.0 - f).reshape(shp), jnp.log(f).reshape(shp)


def hgrn_chunk_scan(q, k, v, log_f, s0):
    bsz, t_len, n_h, _ = q.shape
    n_chunks = t_len // HG_CHUNK

    def to_chunks(a):
        return a.reshape(bsz, n_chunks, HG_CHUNK, n_h, a.shape[-1]).transpose(1, 0, 3, 2, 4)

    lower = jnp.tril(jnp.ones((HG_CHUNK, HG_CHUNK), dtype=bool))[:, :, None]

    def step(state, inp):
        qc, kc, vc, gc = inp
        b = jnp.cumsum(gc, axis=2)
        o = jnp.einsum('bhtd,bhdv->bhtv', qc * jnp.exp(b), state)
        rel = jnp.where(lower, b[:, :, :, None, :] - b[:, :, None, :, :], -jnp.inf)
        att = jnp.einsum('bhtd,bhsd,bhtsd->bhts', qc, kc, jnp.exp(rel))
        o = o + jnp.einsum('bhts,bhsv->bhtv', att, vc)
        b_end = b[:, :, -1]
        state = jnp.exp(b_end)[..., None] * state + jnp.einsum(
            'bhsd,bhsv->bhdv', kc * jnp.exp(b_end[:, :, None] - b), vc)
        return state, o

    s_end, o = lax.scan(step, s0, (to_chunks(q), to_chunks(k), to_chunks(v), to_chunks(log_f)))
    return o.transpose(1, 0, 3, 2, 4).reshape(bsz, t_len, n_h, v.shape[-1]), s_end


def hgrn_bidir(q, v, k_f, lf_f, k_b, lf_b, s_f, s_b):
    o_f, s_f = hgrn_chunk_scan(q, k_f, v, lf_f, s_f)
    o_b, s_b = hgrn_chunk_scan(flip(q), flip(k_b), flip(v), flip(lf_b), s_b)
    return o_f + flip(o_b), s_f, s_b


def block_diag(x, w):
    bsz, t_len, _ = x.shape
    y = jnp.einsum('btnc,ncd->btnd', x.reshape(bsz, t_len, RG_BLOCKS, RG_BW), w)
    return y.reshape(bsz, t_len, RG_W)


def rglru_gates(xc, wa, ba, wx, bx, lam):
    r = jax.nn.sigmoid(block_diag(xc, wa) + ba)
    i = jax.nn.sigmoid(block_diag(xc, wx) + bx)
    log_a = -RG_C * r * jax.nn.softplus(-lam.astype(jnp.float32))
    u = jnp.sqrt(-jnp.expm1(2.0 * log_a)) * (i * xc)
    return log_a, u


def linear_scan(log_a, u, h0):
    a = jnp.exp(log_a)
    u = u.at[:, 0].add(a[:, 0] * h0)

    def combine(l, r):
        return l[0] * r[0], r[0] * l[1] + r[1]

    _, h = lax.associative_scan(combine, (a, u), axis=1)
    return h, h[:, -1]


def rglru_bidir(g_f, g_b, h_f, h_b):
    y_f, h_f = linear_scan(g_f[0], g_f[1], h_f)
    y_b, h_b = linear_scan(flip(g_b[0]), flip(g_b[1]), h_b)
    return y_f + flip(y_b), h_f, h_b


def even_mixer(hl, hc, w_in, w_out, lb, hg_gain, conv_w, conv_b, wa, ba, wx, bx, lam):
    def prepare(h):
        bsz, t_len, _ = h.shape
        q, i, f_fw, f_bw, g, xr, gate = jnp.split(pmm3(h, w_in), EVEN_SPLITS, axis=-1)
        q = jax.nn.silu(q.astype(jnp.float32)).reshape(bsz, t_len, HG_HEADS, HG_DK)
        v = i.astype(jnp.float32).reshape(bsz, t_len, HG_HEADS, HG_DV)
        k_f, lf_f = hgrn_gate(f_fw, lb[0])
        k_b, lf_b = hgrn_gate(f_bw, lb[1])
        xc = centred_dwconv(xr, conv_w, conv_b).astype(jnp.float32)
        g_f = rglru_gates(xc, wa[0], ba[0], wx[0], bx[0], lam[0])
        g_b = rglru_gates(xc, wa[1], ba[1], wx[1], bx[1], lam[1])
        return (q, v, k_f, lf_f, k_b, lf_b), (g_f, g_b), (g, gate)

    def merge(o_hg, h_rg, g, gate):
        bsz, t_len, _ = g.shape
        y_a = rms_norm(o_hg, hg_gain) * jax.nn.silu(g.astype(jnp.float32)).reshape(bsz, t_len, HG_HEADS, HG_DV)
        y_b = jax.nn.gelu(gate.astype(jnp.float32)) * h_rg
        y = jnp.concatenate([y_a.reshape(bsz, t_len, HG_WV), y_b], axis=-1)
        return pmm3(y.astype(g.dtype), w_out)

    hg_c, rg_c, gt_c = prepare(hc)
    hg_l, rg_l, gt_l = prepare(hl)
    bsz = hl.shape[0]
    s0 = jnp.zeros((bsz, HG_HEADS, HG_DK, HG_DV), jnp.float32)
    h0 = jnp.zeros((bsz, RG_W), jnp.float32)
    o_c, s_f, s_b = hgrn_bidir(*hg_c, s0, s0)
    r_c, h_f, h_b = rglru_bidir(*rg_c, h0, h0)
    o_l, _, _ = hgrn_bidir(*hg_l, s_f, s_b)
    r_l, _, _ = rglru_bidir(*rg_l, h_f, h_b)
    return merge(o_l, r_l, *gt_l), merge(o_c, r_c, *gt_c)


def axial_rope(t_len):
    rows = t_len // GRID_W
    row = jnp.repeat(jnp.arange(rows, dtype=jnp.float32), GRID_W)
    col = jnp.tile(jnp.arange(GRID_W, dtype=jnp.float32), rows)
    n_freq = ATT_DH // 4
    inv = ROPE_BASE ** (-jnp.arange(n_freq, dtype=jnp.float32) / n_freq)
    ang = jnp.concatenate([row[:, None] * inv, col[:, None] * inv], axis=-1)
    return jnp.cos(ang), jnp.sin(ang)


def apply_rope(x, cos, sin):
    xf = x.astype(jnp.float32)
    x1, x2 = jnp.split(xf, 2, axis=-1)
    cs, sn = cos[None, :, None, :], sin[None, :, None, :]
    return jnp.concatenate([x1 * cs - x2 * sn, x2 * cs + x1 * sn], axis=-1).astype(x.dtype)


def sink_attention(q, k, v, sink, mask):
    s = jnp.einsum('bqhgd,bkhd->bhgqk', q, k).astype(jnp.float32) * (ATT_DH ** -0.5)
    if mask is not None:
        s = jnp.where(mask, s, -jnp.inf)
    sk = jnp.broadcast_to(sink.astype(jnp.float32)[None, :, :, None, None], s.shape[:-1] + (1,))
    p = jax.nn.softmax(jnp.concatenate([sk, s], axis=-1), axis=-1)[..., 1:]
    return jnp.einsum('bhgqk,bkhd->bqhgd', p.astype(v.dtype), v)


def windowed_attention(q, k, v, kc, vc, sink):
    bsz, t_len = q.shape[:2]
    n_blk = t_len // ATT_BLOCK
    n_ctx = kc.shape[1]
    pad = ((0, 0), (ATT_BLOCK, ATT_BLOCK), (0, 0), (0, 0))
    kp, vp = jnp.pad(k, pad), jnp.pad(v, pad)
    qb = q.reshape(bsz, n_blk, ATT_BLOCK, ATT_HKV, ATT_G, ATT_DH).transpose(1, 0, 2, 3, 4, 5)
    offs_q = jnp.arange(ATT_BLOCK)
    offs_k = jnp.arange(3 * ATT_BLOCK)
    ctx_mask = jnp.ones((ATT_BLOCK, n_ctx), dtype=bool)

    def block(args):
        n, qn = args
        kn = lax.dynamic_slice_in_dim(kp, n * ATT_BLOCK, 3 * ATT_BLOCK, axis=1)
        vn = lax.dynamic_slice_in_dim(vp, n * ATT_BLOCK, 3 * ATT_BLOCK, axis=1)
        qpos = n * ATT_BLOCK + offs_q
        kpos = (n - 1) * ATT_BLOCK + offs_k
        local = (jnp.abs(qpos[:, None] - kpos[None, :]) <= WINDOW) & (kpos >= 0)[None, :] & (kpos < t_len)[None, :]
        mask = jnp.concatenate([ctx_mask, local], axis=1)
        return sink_attention(qn, jnp.concatenate([kc, kn], axis=1), jnp.concatenate([vc, vn], axis=1), sink, mask)

    o = lax.map(block, (jnp.arange(n_blk), qb))
    return o.transpose(1, 0, 2, 3, 4, 5).reshape(bsz, t_len, ATT_WQ)


def fourier_mix(z):
    bsz, t_len, _ = z.shape
    zz = z.astype(jnp.float32).reshape(bsz, t_len, FN_GROUPS, FN_DIM)
    y = jnp.fft.fft2(zz, axes=(1, 3), norm='ortho').real
    return y.reshape(bsz, t_len, FN_W).astype(z.dtype)


def odd_mixer(hl, hc, w_in, w_out, q_gain, k_gain, sink, need_ctx):
    bsz, t_len, _ = hl.shape
    n_ctx = hc.shape[1]
    ql, kl, vl, fl = jnp.split(pmm3(hl, w_in), ODD_SPLITS, axis=-1)
    cos, sin = axial_rope(t_len)
    ql = apply_rope(rms_norm(ql.reshape(bsz, t_len, ATT_HQ, ATT_DH), q_gain), cos, sin)
    kl = apply_rope(rms_norm(kl.reshape(bsz, t_len, ATT_HKV, ATT_DH), k_gain), cos, sin)
    vl = vl.reshape(bsz, t_len, ATT_HKV, ATT_DH)
    kc, vc = jnp.split(pmm3(hc, w_in[:, ATT_WQ:ATT_WQ + 2 * ATT_WKV]), 2, axis=-1)
    kc = rms_norm(kc.reshape(bsz, n_ctx, ATT_HKV, ATT_DH), k_gain)
    vc = vc.reshape(bsz, n_ctx, ATT_HKV, ATT_DH)
    sink_g = sink.reshape(ATT_HKV, ATT_G)
    a_l = windowed_attention(ql.reshape(bsz, t_len, ATT_HKV, ATT_G, ATT_DH), kl, vl, kc, vc, sink_g)
    y_l = pmm3(jnp.concatenate([a_l, fourier_mix(fl)], axis=-1), w_out)
    if not need_ctx:
        return y_l, None
    qc = rms_norm(pmm3(hc, w_in[:, :ATT_WQ]).reshape(bsz, n_ctx, ATT_HKV, ATT_G, ATT_DH), q_gain)
    fc = pmm3(hc, w_in[:, ATT_WQ + 2 * ATT_WKV:])
    a_c = sink_attention(qc, kc, vc, sink_g, None).reshape(bsz, n_ctx, ATT_WQ)
    y_c = pmm3(jnp.concatenate([a_c, fourier_mix(fc)], axis=-1), w_out)
    return y_l, y_c


NT_DIMS = (((1,), (1,)), ((), ()))
PEER_TA = 256
PEER_TG = 8
PEER_GPB = 2
PEER_SUB = 8
PEER_COLS = PK_HEADS * PK_TOPK * PEER_SUB
PEER_CHUNK = 4096
SC_W = 32
SC_IDX_BLOCK = 128


def _lane_group_allsum(y, lane):
    k = 1
    while k < PEER_SUB:
        up = pltpu.roll(y, 128 - k, axis=1)
        dn = pltpu.roll(y, k, axis=1)
        y = y + jnp.where((lane & k) == 0, up, dn)
        k *= 2
    return y


def _mix_kernel(x2_ref, gate_ref, g_ref, o2_ref, hp_ref):
    lane = lax.broadcasted_iota(jnp.int32, (PEER_SUB, PEER_COLS), 1)
    sub = lax.broadcasted_iota(jnp.int32, (PEER_SUB, PEER_COLS), 0)
    same_s = (lane % PEER_SUB) == sub
    lane128 = lax.broadcasted_iota(jnp.int32, (PEER_TG, 128), 1)
    for g in range(PEER_GPB):
        for t in range(PEER_TG):
            tok = g * PEER_TG + t
            w = g_ref[pl.ds(tok * PEER_COLS, PEER_COLS), :]
            ub = lax.bitcast_convert_type(w << 16, jnp.float32).astype(jnp.bfloat16)
            x2 = x2_ref[pl.ds(tok * PEER_SUB, PEER_SUB), :].astype(jnp.bfloat16)
            p = lax.dot_general(x2, ub, NT_DIMS, preferred_element_type=jnp.float32)
            hp_ref[pl.ds(t, 1), :] = jnp.sum(jnp.where(same_s, p, 0.0), axis=0, keepdims=True)
        hp = hp_ref[...]
        hfull = jnp.concatenate(
            [_lane_group_allsum(hp[:, c * 128:(c + 1) * 128], lane128) for c in range(PEER_COLS // 128)], axis=1)
        act = jax.nn.gelu(hfull) * gate_ref[pl.ds(g * PEER_TG, PEER_TG), :]
        for t in range(PEER_TG):
            tok = g * PEER_TG + t
            a2 = jnp.where(same_s, act[t:t + 1, :], 0.0).astype(jnp.bfloat16)
            w = g_ref[pl.ds(tok * PEER_COLS, PEER_COLS), :]
            vb = lax.bitcast_convert_type(w & jnp.int32(-65536), jnp.float32).astype(jnp.bfloat16)
            o2_ref[pl.ds(tok * PEER_SUB, PEER_SUB), :] = jnp.dot(a2, vb, preferred_element_type=jnp.float32)


def peer_mix(x, gate, gathered):
    n, d = x.shape
    tb = PEER_TG * PEER_GPB
    x2 = x.reshape(n * PEER_SUB, 128)
    gate8 = jnp.repeat(gate, PEER_SUB, axis=1)
    g2 = gathered.reshape(n * PEER_COLS, 128)
    out2 = pl.pallas_call(
        _mix_kernel,
        out_shape=jax.ShapeDtypeStruct((n * PEER_SUB, 128), jnp.float32),
        grid=(n // tb,),
        in_specs=[pl.BlockSpec((tb * PEER_SUB, 128), lambda i: (i, 0)),
                  pl.BlockSpec((tb, PEER_COLS), lambda i: (i, 0)),
                  pl.BlockSpec((tb * PEER_COLS, 128), lambda i: (i, 0))],
        out_specs=pl.BlockSpec((tb * PEER_SUB, 128), lambda i: (i, 0)),
        scratch_shapes=[pltpu.VMEM((PEER_TG, PEER_COLS), jnp.float32)],
        compiler_params=pltpu.CompilerParams(dimension_semantics=("parallel",),
                                             vmem_limit_bytes=48 * 1024 * 1024),
        name="peer_mix",
    )(x2, gate8, g2)
    return out2.reshape(n, d)


def sc_gather(table, idxp):
    nblk = idxp.shape[1] // SC_IDX_BLOCK
    mesh = plsc.VectorSubcoreMesh(core_axis_name="c", subcore_axis_name="s")

    @functools.partial(pl.kernel, out_type=jax.ShapeDtypeStruct((nblk * SC_W, 8, 128), table.dtype), mesh=mesh)
    def gather_kernel(t_hbm, i_hbm, o_hbm):
        def body(i_vmem, o_vmem):
            pltpu.sync_copy(t_hbm.at[i_vmem.at[0, pl.ds(0, SC_W)]], o_vmem)

        pltpu.emit_pipeline(
            body, grid=(nblk,),
            in_specs=[pl.BlockSpec((1, SC_IDX_BLOCK), lambda i: (0, i))],
            out_specs=[pl.BlockSpec((SC_W, 8, 128), lambda i: (i, 0, 0))],
            core_axis_name=("c", "s"),
            dimension_semantics=(pltpu.PARALLEL,),
            trace_scopes=False,
        )(i_hbm, o_hbm)

    return gather_kernel(table, idxp)


def pack_table(u, v):
    ub = lax.bitcast_convert_type(u.astype(jnp.bfloat16), jnp.uint16).astype(jnp.uint32)
    vb = lax.bitcast_convert_type(v.astype(jnp.bfloat16), jnp.uint16).astype(jnp.uint32)
    w = lax.bitcast_convert_type(ub | (vb << 16), jnp.int32)
    return w.reshape(u.shape[0], 8, 128)


def topk_last(s, k):
    n = s.shape[-1]
    iota = lax.broadcasted_iota(jnp.int32, s.shape, s.ndim - 1)
    vals, idxs = [], []
    for _ in range(k):
        m = jnp.max(s, axis=-1, keepdims=True)
        am = jnp.min(jnp.where(s == m, iota, n), axis=-1, keepdims=True)
        vals.append(m)
        idxs.append(am)
        s = jnp.where(iota == am, -jnp.inf, s)
    return jnp.concatenate(vals, axis=-1), jnp.concatenate(idxs, axis=-1)


def peer(h, w_q, b_q, sub_keys, table):
    n, d = h.shape
    qq = (pmm(h, w_q) + b_q).reshape(n, PK_HEADS, 2, PK_DKH)
    ss = jnp.einsum('thpd,hpkd->thpk', qq, sub_keys.astype(jnp.float32))
    sv, si = topk_last(ss, PK_TOPK)
    cand = (sv[:, :, 0, :, None] + sv[:, :, 1, None, :]).reshape(n, PK_HEADS, PK_TOPK * PK_TOPK)
    cidx = (si[:, :, 0, :, None] * PK_NKEYS + si[:, :, 1, None, :]).reshape(n, PK_HEADS, PK_TOPK * PK_TOPK)
    best, pos = topk_last(cand, PK_TOPK)
    eidx = jnp.take_along_axis(cidx, pos, axis=-1).reshape(n, PK_HEADS * PK_TOPK)
    gate = jax.nn.softmax(best, axis=-1).reshape(n, PK_HEADS * PK_TOPK)
    outs = []
    for c0 in range(0, n, PEER_CHUNK):
        c1 = min(n, c0 + PEER_CHUNK)
        e = eidx[c0:c1].reshape(-1, SC_W)
        idxp = jnp.pad(e, ((0, 0), (0, SC_IDX_BLOCK - SC_W))).reshape(1, -1)
        gathered = sc_gather(table, idxp)
        outs.append(peer_mix(h[c0:c1], gate[c0:c1], gathered))
    return jnp.concatenate(outs, axis=0)


def kernel(x, c, ctx, c_ctx, w_mod, b_mod, g_mix, g_ffn, e_w_in, e_w_out, hg_lb_logits, hg_gain, rg_conv_w,
           rg_conv_b, rg_wa, rg_ba, rg_wx, rg_bx, rg_lambda, o_w_in, o_w_out, q_gain, k_gain, sinks, p_wq,
           p_bq, p_keys, p_u, p_v):
    lb_all = jnp.cumsum(jax.nn.softmax(hg_lb_logits.astype(jnp.float32), axis=1), axis=1)
    tables = [pack_table(p_u[l], p_v[l]) for l in range(DEPTH)]
    lat, cx = x, ctx
    for l in range(DEPTH):
        last = l == DEPTH - 1
        j = l // 2
        m_lat = jnp.split((jax.nn.silu(c) @ w_mod[l] + b_mod[l])[:, None, :], N_MOD, axis=-1)
        m_ctx = jnp.split(jax.nn.silu(c_ctx) @ w_mod[l] + b_mod[l], N_MOD, axis=-1)
        hl = modulate(rms_norm(lat, g_mix[l]), m_lat[0], m_lat[1])
        hc = modulate(rms_norm(cx, g_mix[l]), m_ctx[0], m_ctx[1])
        if l % 2 == 0:
            yl, yc = even_mixer(hl, hc, e_w_in[j], e_w_out[j], lb_all[:, j], hg_gain[j], rg_conv_w[j], rg_conv_b[j],
                                rg_wa[j], rg_ba[j], rg_wx[j], rg_bx[j], rg_lambda[j])
        else:
            yl, yc = odd_mixer(hl, hc, o_w_in[j], o_w_out[j], q_gain[j], k_gain[j], sinks[j], not last)
        lat = lat + m_lat[2] * yl
        hl2 = modulate(rms_norm(lat, g_ffn[l]), m_lat[3], m_lat[4])
        n_lat = hl2.shape[0] * hl2.shape[1]
        if last:
            ff = peer(hl2.reshape(n_lat, hl2.shape[-1]), p_wq[l], p_bq[l], p_keys[l], tables[l])
            lat = lat + m_lat[5] * ff.reshape(lat.shape)
        else:
            cx = cx + m_ctx[2] * yc
            hc2 = modulate(rms_norm(cx, g_ffn[l]), m_ctx[3], m_ctx[4])
            tokens = jnp.concatenate([hl2.reshape(n_lat, hl2.shape[-1]), hc2.reshape(-1, hc2.shape[-1])], axis=0)
            ff = peer(tokens, p_wq[l], p_bq[l], p_keys[l], tables[l])
            lat = lat + m_lat[5] * ff[:n_lat].reshape(lat.shape)
            cx = cx + m_ctx[5] * ff[n_lat:].reshape(cx.shape)
    return lat
```

```python
import functools
import math

import jax
import jax.numpy as jnp
import numpy as np
from jax import lax
from jax.experimental import pallas as pl
from jax.experimental.pallas import tpu as pltpu
from jax.experimental.pallas import tpu_sc as plsc

D_MODEL = 1024
DEPTH = 2
GRID_W = 64
EPS = 1e-6
N_MOD = 6

HG_HEADS = 4
HG_DK = 128
HG_DV = 128
HG_WK = HG_HEADS * HG_DK
HG_WV = HG_HEADS * HG_DV
HG_CHUNK = 64
RG_W = 512
RG_BLOCKS = 8
RG_BW = RG_W // RG_BLOCKS
RG_C = 8.0
CONV_W = 4
EVEN_SPLITS = (HG_WK, HG_WK + HG_WV, 2 * HG_WK + HG_WV, 3 * HG_WK + HG_WV, 3 * HG_WK + 2 * HG_WV,
               3 * HG_WK + 2 * HG_WV + RG_W)

ATT_HQ = 12
ATT_HKV = 4
ATT_G = ATT_HQ // ATT_HKV
ATT_DH = 64
ATT_WQ = ATT_HQ * ATT_DH
ATT_WKV = ATT_HKV * ATT_DH
WINDOW = 128
ATT_BLOCK = 128
ROPE_BASE = 10000.0
FN_GROUPS = 4
FN_DIM = 64
FN_W = FN_GROUPS * FN_DIM
ODD_SPLITS = (ATT_WQ, ATT_WQ + ATT_WKV, ATT_WQ + 2 * ATT_WKV)

PK_HEADS = 8
PK_NKEYS = 128
PK_DK = 256
PK_DKH = PK_DK // 2
PK_TOPK = 16
PK_CHUNK = 128


def _mm_kernel(a_ref, b_ref, o_ref):
    o_ref[...] = jnp.dot(a_ref[...].astype(jnp.bfloat16), b_ref[...].astype(jnp.bfloat16),
                         preferred_element_type=jnp.float32)


def pmm(a, b, tm=512, tn=512):
    m, k = a.shape
    n = b.shape[1]
    tm = next(t for t in (tm, 256, 128, m) if m % t == 0)
    tn = next(t for t in (tn, 256, 128, n) if n % t == 0)
    return pl.pallas_call(
        _mm_kernel,
        out_shape=jax.ShapeDtypeStruct((m, n), jnp.float32),
        grid=(m // tm, n // tn),
        in_specs=[pl.BlockSpec((tm, k), lambda i, j: (i, 0)),
                  pl.BlockSpec((k, tn), lambda i, j: (0, j))],
        out_specs=pl.BlockSpec((tm, tn), lambda i, j: (i, j)),
        compiler_params=pltpu.CompilerParams(dimension_semantics=("parallel", "parallel")),
        name="pmm",
    )(a, b)


def pmm3(a, b):
    bsz, t, k = a.shape
    return pmm(a.reshape(bsz * t, k), b).reshape(bsz, t, b.shape[1])


def rms_norm(x, g):
    xf = x.astype(jnp.float32)
    y = xf * lax.rsqrt(jnp.mean(xf * xf, axis=-1, keepdims=True) + EPS)
    return (y * g.astype(jnp.float32)).astype(x.dtype)


def modulate(h, shift, scale):
    return h * (1.0 + scale) + shift


def flip(a):
    return jnp.flip(a, axis=1)


def centred_dwconv(x, w, b):
    left = CONV_W // 2
    y = lax.conv_general_dilated(x, w.astype(x.dtype)[:, None, :], (1,), [(left, CONV_W - 1 - left)],
                                 dimension_numbers=('NWC', 'WIO', 'NWC'), feature_group_count=x.shape[-1])
    return y + b.astype(x.dtype)


HG_SUB = 16
NT_DIMS = (((1,), (1,)), ((), ()))
TN_DIMS = (((0,), (0,)), ((), ()))
HG_QCOL, HG_ICOL, HG_FCOL = 0, HG_HEADS, 2 * HG_HEADS


def _split3(x):
    hi = x.astype(jnp.bfloat16)
    r = x - hi.astype(jnp.float32)
    mid = r.astype(jnp.bfloat16)
    lo = (r - mid.astype(jnp.float32)).astype(jnp.bfloat16)
    return hi, mid, lo


def _dot01(m01, x):
    return sum(jnp.dot(m01, p, preferred_element_type=jnp.float32) for p in _split3(x))


def _hgrn_kernel(qr_ref, ir_ref, fr_ref, lb_ref, s0_ref, o_ref, send_ref, st_scr, *, reverse, n_chunks):
    c = pl.program_id(2)

    @pl.when(c == 0)
    def _():
        st_scr[...] = s0_ref[0, 0]

    n = HG_CHUNK
    row = lax.broadcasted_iota(jnp.int32, (n, n), 0)
    col = lax.broadcasted_iota(jnp.int32, (n, n), 1)
    lb = lb_ref[...]
    f = lb + (1.0 - lb) * jax.nn.sigmoid(fr_ref[0])
    k = 1.0 - f
    g = jnp.log(f)
    q = jax.nn.silu(qr_ref[0])
    v = ir_ref[0]
    if reverse:
        flipm = (row + col == n - 1).astype(jnp.bfloat16)
        q, k, v, g = (_dot01(flipm, a) for a in (q, k, v, g))
    b = _dot01((col <= row).astype(jnp.bfloat16), g)
    st = st_scr[...]
    o = lax.dot_general((q * jnp.exp(b)).astype(jnp.bfloat16), st.astype(jnp.bfloat16), NT_DIMS,
                        preferred_element_type=jnp.float32)
    srow = lax.broadcasted_iota(jnp.int32, (HG_SUB, HG_DK), 0)
    blocks = []
    for r0 in range(0, n, HG_SUB):
        b_i, q_i, k_i, v_i = (a[r0:r0 + HG_SUB] for a in (b, q, k, v))
        o_i = o[r0:r0 + HG_SUB]
        if r0 > 0:
            ref = b[r0 - 1:r0]
            a_q = (q_i * jnp.exp(b_i - ref)).astype(jnp.bfloat16)
            a_k = (k[0:r0] * jnp.exp(ref - b[0:r0])).astype(jnp.bfloat16)
            att = lax.dot_general(a_q, a_k, NT_DIMS, preferred_element_type=jnp.float32)
            o_i = o_i + jnp.dot(att.astype(jnp.bfloat16), v[0:r0].astype(jnp.bfloat16),
                                preferred_element_type=jnp.float32)
        diag = jnp.zeros((HG_SUB, HG_DV), jnp.float32)
        for t in range(HG_SUB):
            e = jnp.exp(jnp.minimum(b_i[t:t + 1] - b_i, 0.0))
            w = jnp.sum(jnp.where(srow <= t, q_i[t:t + 1] * k_i * e, 0.0), axis=1, keepdims=True)
            diag = jnp.where(srow == t, jnp.sum(w * v_i, axis=0, keepdims=True), diag)
        blocks.append(o_i + diag)
    o_full = jnp.concatenate(blocks, axis=0)
    if reverse:
        o_full = _dot01(flipm, o_full)
    o_ref[0] = o_full
    b_end = b[n - 1:n]
    k_dec = (k * jnp.exp(b_end - b)).astype(jnp.bfloat16)
    st_new = st * jnp.exp(b_end) + lax.dot_general(v.astype(jnp.bfloat16), k_dec, TN_DIMS,
                                                   preferred_element_type=jnp.float32)
    st_scr[...] = st_new

    @pl.when(c == n_chunks - 1)
    def _():
        send_ref[0, 0] = st_new


def hgrn_scan(proj, lb, s0t, reverse):
    bsz, t_len, _ = proj.shape
    n_chunks = t_len // HG_CHUNK
    fcol = HG_FCOL + (HG_HEADS if reverse else 0)

    def tmap(c):
        return n_chunks - 1 - c if reverse else c

    blk = (1, HG_CHUNK, HG_DK)
    return pl.pallas_call(
        functools.partial(_hgrn_kernel, reverse=reverse, n_chunks=n_chunks),
        out_shape=(jax.ShapeDtypeStruct((bsz, t_len, HG_HEADS * HG_DV), jnp.float32),
                   jax.ShapeDtypeStruct((bsz, HG_HEADS, HG_DV, HG_DK), jnp.float32)),
        grid=(bsz, HG_HEADS, n_chunks),
        in_specs=[pl.BlockSpec(blk, lambda b, h, c: (b, tmap(c), HG_QCOL + h)),
                  pl.BlockSpec(blk, lambda b, h, c: (b, tmap(c), HG_ICOL + h)),
                  pl.BlockSpec(blk, lambda b, h, c: (b, tmap(c), fcol + h)),
                  pl.BlockSpec((1, HG_DK), lambda b, h, c: (0, h)),
                  pl.BlockSpec((1, 1, HG_DV, HG_DK), lambda b, h, c: (b, h, 0, 0))],
        out_specs=(pl.BlockSpec(blk, lambda b, h, c: (b, tmap(c), h)),
                   pl.BlockSpec((1, 1, HG_DV, HG_DK), lambda b, h, c: (b, h, 0, 0))),
        scratch_shapes=[pltpu.VMEM((HG_DV, HG_DK), jnp.float32)],
        compiler_params=pltpu.CompilerParams(dimension_semantics=("parallel", "parallel", "arbitrary")),
        name="hgrn_scan",
    )(proj, proj, proj, lb, s0t)


def block_diag(x, w):
    bsz, t_len, _ = x.shape
    y = jnp.einsum('btnc,ncd->btnd', x.reshape(bsz, t_len, RG_BLOCKS, RG_BW), w)
    return y.reshape(bsz, t_len, RG_W)


def rglru_gates(xc, wa, ba, wx, bx, lam):
    r = jax.nn.sigmoid(block_diag(xc, wa) + ba)
    i = jax.nn.sigmoid(block_diag(xc, wx) + bx)
    log_a = -RG_C * r * jax.nn.softplus(-lam.astype(jnp.float32))
    u = jnp.sqrt(-jnp.expm1(2.0 * log_a)) * (i * xc)
    return log_a, u


def linear_scan(log_a, u, h0):
    a = jnp.exp(log_a)
    u = u.at[:, 0].add(a[:, 0] * h0)

    def combine(l, r):
        return l[0] * r[0], r[0] * l[1] + r[1]

    _, h = lax.associative_scan(combine, (a, u), axis=1)
    return h, h[:, -1]


def rglru_bidir(g_f, g_b, h_f, h_b):
    y_f, h_f = linear_scan(g_f[0], g_f[1], h_f)
    y_b, h_b = linear_scan(flip(g_b[0]), flip(g_b[1]), h_b)
    return y_f + flip(y_b), h_f, h_b


def even_mixer(hl, hc, w_in, w_out, lb, hg_gain, conv_w, conv_b, wa, ba, wx, bx, lam):
    def prepare(h):
        proj = pmm3(h, w_in)
        _, _, _, _, g, xr, gate = jnp.split(proj, EVEN_SPLITS, axis=-1)
        xc = centred_dwconv(xr, conv_w, conv_b).astype(jnp.float32)
        g_f = rglru_gates(xc, wa[0], ba[0], wx[0], bx[0], lam[0])
        g_b = rglru_gates(xc, wa[1], ba[1], wx[1], bx[1], lam[1])
        return proj, (g_f, g_b), (g, gate)

    def hgrn_bidir(proj, s_f, s_b):
        o_f, s_f = hgrn_scan(proj, lb[0:1], s_f, False)
        o_b, s_b = hgrn_scan(proj, lb[1:2], s_b, True)
        return o_f + o_b, s_f, s_b

    def merge(o_hg, h_rg, g, gate):
        bsz, t_len, _ = g.shape
        o_hg = o_hg.reshape(bsz, t_len, HG_HEADS, HG_DV)
        y_a = rms_norm(o_hg, hg_gain) * jax.nn.silu(g.astype(jnp.float32)).reshape(bsz, t_len, HG_HEADS, HG_DV)
        y_b = jax.nn.gelu(gate.astype(jnp.float32)) * h_rg
        y = jnp.concatenate([y_a.reshape(bsz, t_len, HG_WV), y_b], axis=-1)
        return pmm3(y.astype(g.dtype), w_out)

    hg_c, rg_c, gt_c = prepare(hc)
    hg_l, rg_l, gt_l = prepare(hl)
    bsz = hl.shape[0]
    s0 = jnp.zeros((bsz, HG_HEADS, HG_DK, HG_DV), jnp.float32)
    h0 = jnp.zeros((bsz, RG_W), jnp.float32)
    o_c, s_f, s_b = hgrn_bidir(hg_c, s0, s0)
    r_c, h_f, h_b = rglru_bidir(*rg_c, h0, h0)
    o_l, _, _ = hgrn_bidir(hg_l, s_f, s_b)
    r_l, _, _ = rglru_bidir(*rg_l, h_f, h_b)
    return merge(o_l, r_l, *gt_l), merge(o_c, r_c, *gt_c)


def axial_rope(t_len):
    rows = t_len // GRID_W
    row = jnp.repeat(jnp.arange(rows, dtype=jnp.float32), GRID_W)
    col = jnp.tile(jnp.arange(GRID_W, dtype=jnp.float32), rows)
    n_freq = ATT_DH // 4
    inv = ROPE_BASE ** (-jnp.arange(n_freq, dtype=jnp.float32) / n_freq)
    ang = jnp.concatenate([row[:, None] * inv, col[:, None] * inv], axis=-1)
    return jnp.cos(ang), jnp.sin(ang)


def apply_rope(x, cos, sin):
    xf = x.astype(jnp.float32)
    x1, x2 = jnp.split(xf, 2, axis=-1)
    cs, sn = cos[None, :, None, :], sin[None, :, None, :]
    return jnp.concatenate([x1 * cs - x2 * sn, x2 * cs + x1 * sn], axis=-1).astype(x.dtype)


def sink_attention(q, k, v, sink, mask):
    s = jnp.einsum('bqhgd,bkhd->bhgqk', q, k).astype(jnp.float32) * (ATT_DH ** -0.5)
    if mask is not None:
        s = jnp.where(mask, s, -jnp.inf)
    sk = jnp.broadcast_to(sink.astype(jnp.float32)[None, :, :, None, None], s.shape[:-1] + (1,))
    p = jax.nn.softmax(jnp.concatenate([sk, s], axis=-1), axis=-1)[..., 1:]
    return jnp.einsum('bhgqk,bkhd->bqhgd', p.astype(v.dtype), v)


def windowed_attention(q, k, v, kc, vc, sink):
    bsz, t_len = q.shape[:2]
    n_blk = t_len // ATT_BLOCK
    n_ctx = kc.shape[1]
    pad = ((0, 0), (ATT_BLOCK, ATT_BLOCK), (0, 0), (0, 0))
    kp, vp = jnp.pad(k, pad), jnp.pad(v, pad)
    qb = q.reshape(bsz, n_blk, ATT_BLOCK, ATT_HKV, ATT_G, ATT_DH).transpose(1, 0, 2, 3, 4, 5)
    offs_q = jnp.arange(ATT_BLOCK)
    offs_k = jnp.arange(3 * ATT_BLOCK)
    ctx_mask = jnp.ones((ATT_BLOCK, n_ctx), dtype=bool)

    def block(args):
        n, qn = args
        kn = lax.dynamic_slice_in_dim(kp, n * ATT_BLOCK, 3 * ATT_BLOCK, axis=1)
        vn = lax.dynamic_slice_in_dim(vp, n * ATT_BLOCK, 3 * ATT_BLOCK, axis=1)
        qpos = n * ATT_BLOCK + offs_q
        kpos = (n - 1) * ATT_BLOCK + offs_k
        local = (jnp.abs(qpos[:, None] - kpos[None, :]) <= WINDOW) & (kpos >= 0)[None, :] & (kpos < t_len)[None, :]
        mask = jnp.concatenate([ctx_mask, local], axis=1)
        return sink_attention(qn, jnp.concatenate([kc, kn], axis=1), jnp.concatenate([vc, vn], axis=1), sink, mask)

    o = lax.map(block, (jnp.arange(n_blk), qb))
    return o.transpose(1, 0, 2, 3, 4, 5).reshape(bsz, t_len, ATT_WQ)


def fourier_mix(z):
    bsz, t_len, _ = z.shape
    zz = z.astype(jnp.float32).reshape(bsz, t_len, FN_GROUPS, FN_DIM)
    y = jnp.fft.fft2(zz, axes=(1, 3), norm='ortho').real
    return y.reshape(bsz, t_len, FN_W).astype(z.dtype)


def odd_mixer(hl, hc, w_in, w_out, q_gain, k_gain, sink, need_ctx):
    bsz, t_len, _ = hl.shape
    n_ctx = hc.shape[1]
    ql, kl, vl, fl = jnp.split(pmm3(hl, w_in), ODD_SPLITS, axis=-1)
    cos, sin = axial_rope(t_len)
    ql = apply_rope(rms_norm(ql.reshape(bsz, t_len, ATT_HQ, ATT_DH), q_gain), cos, sin)
    kl = apply_rope(rms_norm(kl.reshape(bsz, t_len, ATT_HKV, ATT_DH), k_gain), cos, sin)
    vl = vl.reshape(bsz, t_len, ATT_HKV, ATT_DH)
    kc, vc = jnp.split(pmm3(hc, w_in[:, ATT_WQ:ATT_WQ + 2 * ATT_WKV]), 2, axis=-1)
    kc = rms_norm(kc.reshape(bsz, n_ctx, ATT_HKV, ATT_DH), k_gain)
    vc = vc.reshape(bsz, n_ctx, ATT_HKV, ATT_DH)
    sink_g = sink.reshape(ATT_HKV, ATT_G)
    a_l = windowed_attention(ql.reshape(bsz, t_len, ATT_HKV, ATT_G, ATT_DH), kl, vl, kc, vc, sink_g)
    y_l = pmm3(jnp.concatenate([a_l, fourier_mix(fl)], axis=-1), w_out)
    if not need_ctx:
        return y_l, None
    qc = rms_norm(pmm3(hc, w_in[:, :ATT_WQ]).reshape(bsz, n_ctx, ATT_HKV, ATT_G, ATT_DH), q_gain)
    fc = pmm3(hc, w_in[:, ATT_WQ + 2 * ATT_WKV:])
    a_c = sink_attention(qc, kc, vc, sink_g, None).reshape(bsz, n_ctx, ATT_WQ)
    y_c = pmm3(jnp.concatenate([a_c, fourier_mix(fc)], axis=-1), w_out)
    return y_l, y_c


PEER_TA = 256
PEER_TG = 8
PEER_GPB = 2
PEER_SUB = 8
PEER_COLS = PK_HEADS * PK_TOPK * PEER_SUB
PEER_CHUNK = 4096
SC_W = 32
SC_IDX_BLOCK = 128


def _lane_group_allsum(y, lane):
    k = 1
    while k < PEER_SUB:
        up = pltpu.roll(y, 128 - k, axis=1)
        dn = pltpu.roll(y, k, axis=1)
        y = y + jnp.where((lane & k) == 0, up, dn)
        k *= 2
    return y


def _mix_kernel(x2_ref, gate_ref, g_ref, o2_ref, hp_ref):
    lane = lax.broadcasted_iota(jnp.int32, (PEER_SUB, PEER_COLS), 1)
    sub = lax.broadcasted_iota(jnp.int32, (PEER_SUB, PEER_COLS), 0)
    same_s = (lane % PEER_SUB) == sub
    lane128 = lax.broadcasted_iota(jnp.int32, (PEER_TG, 128), 1)
    for g in range(PEER_GPB):
        for t in range(PEER_TG):
            tok = g * PEER_TG + t
            w = g_ref[pl.ds(tok * PEER_COLS, PEER_COLS), :]
            ub = lax.bitcast_convert_type(w << 16, jnp.float32).astype(jnp.bfloat16)
            x2 = x2_ref[pl.ds(tok * PEER_SUB, PEER_SUB), :].astype(jnp.bfloat16)
            p = lax.dot_general(x2, ub, NT_DIMS, preferred_element_type=jnp.float32)
            hp_ref[pl.ds(t, 1), :] = jnp.sum(jnp.where(same_s, p, 0.0), axis=0, keepdims=True)
        hp = hp_ref[...]
        hfull = jnp.concatenate(
            [_lane_group_allsum(hp[:, c * 128:(c + 1) * 128], lane128) for c in range(PEER_COLS // 128)], axis=1)
        act = jax.nn.gelu(hfull) * gate_ref[pl.ds(g * PEER_TG, PEER_TG), :]
        for t in range(PEER_TG):
            tok = g * PEER_TG + t
            a2 = jnp.where(same_s, act[t:t + 1, :], 0.0).astype(jnp.bfloat16)
            w = g_ref[pl.ds(tok * PEER_COLS, PEER_COLS), :]
            vb = lax.bitcast_convert_type(w & jnp.int32(-65536), jnp.float32).astype(jnp.bfloat16)
            o2_ref[pl.ds(tok * PEER_SUB, PEER_SUB), :] = jnp.dot(a2, vb, preferred_element_type=jnp.float32)


def peer_mix(x, gate, gathered):
    n, d = x.shape
    tb = PEER_TG * PEER_GPB
    x2 = x.reshape(n * PEER_SUB, 128)
    gate8 = jnp.repeat(gate, PEER_SUB, axis=1)
    g2 = gathered.reshape(n * PEER_COLS, 128)
    out2 = pl.pallas_call(
        _mix_kernel,
        out_shape=jax.ShapeDtypeStruct((n * PEER_SUB, 128), jnp.float32),
        grid=(n // tb,),
        in_specs=[pl.BlockSpec((tb * PEER_SUB, 128), lambda i: (i, 0)),
                  pl.BlockSpec((tb, PEER_COLS), lambda i: (i, 0)),
                  pl.BlockSpec((tb * PEER_COLS, 128), lambda i: (i, 0))],
        out_specs=pl.BlockSpec((tb * PEER_SUB, 128), lambda i: (i, 0)),
        scratch_shapes=[pltpu.VMEM((PEER_TG, PEER_COLS), jnp.float32)],
        compiler_params=pltpu.CompilerParams(dimension_semantics=("parallel",),
                                             vmem_limit_bytes=48 * 1024 * 1024),
        name="peer_mix",
    )(x2, gate8, g2)
    return out2.reshape(n, d)


def sc_gather(table, idxp):
    nblk = idxp.shape[1] // SC_IDX_BLOCK
    mesh = plsc.VectorSubcoreMesh(core_axis_name="c", subcore_axis_name="s")

    @functools.partial(pl.kernel, out_type=jax.ShapeDtypeStruct((nblk * SC_W, 8, 128), table.dtype), mesh=mesh)
    def gather_kernel(t_hbm, i_hbm, o_hbm):
        def body(i_vmem, o_vmem):
            pltpu.sync_copy(t_hbm.at[i_vmem.at[0, pl.ds(0, SC_W)]], o_vmem)

        pltpu.emit_pipeline(
            body, grid=(nblk,),
            in_specs=[pl.BlockSpec((1, SC_IDX_BLOCK), lambda i: (0, i))],
            out_specs=[pl.BlockSpec((SC_W, 8, 128), lambda i: (i, 0, 0))],
            core_axis_name=("c", "s"),
            dimension_semantics=(pltpu.PARALLEL,),
            trace_scopes=False,
        )(i_hbm, o_hbm)

    return gather_kernel(table, idxp)


def pack_table(u, v):
    ub = lax.bitcast_convert_type(u.astype(jnp.bfloat16), jnp.uint16).astype(jnp.uint32)
    vb = lax.bitcast_convert_type(v.astype(jnp.bfloat16), jnp.uint16).astype(jnp.uint32)
    w = lax.bitcast_convert_type(ub | (vb << 16), jnp.int32)
    return w.reshape(u.shape[0], 8, 128)


def topk_last(s, k):
    n = s.shape[-1]
    iota = lax.broadcasted_iota(jnp.int32, s.shape, s.ndim - 1)
    vals, idxs = [], []
    for _ in range(k):
        m = jnp.max(s, axis=-1, keepdims=True)
        am = jnp.min(jnp.where(s == m, iota, n), axis=-1, keepdims=True)
        vals.append(m)
        idxs.append(am)
        s = jnp.where(iota == am, -jnp.inf, s)
    return jnp.concatenate(vals, axis=-1), jnp.concatenate(idxs, axis=-1)


def peer(h, w_q, b_q, sub_keys, table):
    n, d = h.shape
    qq = (pmm(h, w_q) + b_q).reshape(n, PK_HEADS, 2, PK_DKH)
    ss = jnp.einsum('thpd,hpkd->thpk', qq, sub_keys.astype(jnp.float32))
    sv, si = topk_last(ss, PK_TOPK)
    cand = (sv[:, :, 0, :, None] + sv[:, :, 1, None, :]).reshape(n, PK_HEADS, PK_TOPK * PK_TOPK)
    cidx = (si[:, :, 0, :, None] * PK_NKEYS + si[:, :, 1, None, :]).reshape(n, PK_HEADS, PK_TOPK * PK_TOPK)
    best, pos = topk_last(cand, PK_TOPK)
    eidx = jnp.take_along_axis(cidx, pos, axis=-1).reshape(n, PK_HEADS * PK_TOPK)
    gate = jax.nn.softmax(best, axis=-1).reshape(n, PK_HEADS * PK_TOPK)
    outs = []
    for c0 in range(0, n, PEER_CHUNK):
        c1 = min(n, c0 + PEER_CHUNK)
        e = eidx[c0:c1].reshape(-1, SC_W)
        idxp = jnp.pad(e, ((0, 0), (0, SC_IDX_BLOCK - SC_W))).reshape(1, -1)
        gathered = sc_gather(table, idxp)
        outs.append(peer_mix(h[c0:c1], gate[c0:c1], gathered))
    return jnp.concatenate(outs, axis=0)


def kernel(x, c, ctx, c_ctx, w_mod, b_mod, g_mix, g_ffn, e_w_in, e_w_out, hg_lb_logits, hg_gain, rg_conv_w,
           rg_conv_b, rg_wa, rg_ba, rg_wx, rg_bx, rg_lambda, o_w_in, o_w_out, q_gain, k_gain, sinks, p_wq,
           p_bq, p_keys, p_u, p_v):
    lb_all = jnp.cumsum(jax.nn.softmax(hg_lb_logits.astype(jnp.float32), axis=1), axis=1)
    tables = [pack_table(p_u[l], p_v[l]) for l in range(DEPTH)]
    lat, cx = x, ctx
    for l in range(DEPTH):
        last = l == DEPTH - 1
        j = l // 2
        m_lat = jnp.split((jax.nn.silu(c) @ w_mod[l] + b_mod[l])[:, None, :], N_MOD, axis=-1)
        m_ctx = jnp.split(jax.nn.silu(c_ctx) @ w_mod[l] + b_mod[l], N_MOD, axis=-1)
        hl = modulate(rms_norm(lat, g_mix[l]), m_lat[0], m_lat[1])
        hc = modulate(rms_norm(cx, g_mix[l]), m_ctx[0], m_ctx[1])
        if l % 2 == 0:
            yl, yc = even_mixer(hl, hc, e_w_in[j], e_w_out[j], lb_all[:, j], hg_gain[j], rg_conv_w[j], rg_conv_b[j],
                                rg_wa[j], rg_ba[j], rg_wx[j], rg_bx[j], rg_lambda[j])
        else:
            yl, yc = odd_mixer(hl, hc, o_w_in[j], o_w_out[j], q_gain[j], k_gain[j], sinks[j], not last)
        lat = lat + m_lat[2] * yl
        hl2 = modulate(rms_norm(lat, g_ffn[l]), m_lat[3], m_lat[4])
        n_lat = hl2.shape[0] * hl2.shape[1]
        if last:
            ff = peer(hl2.reshape(n_lat, hl2.shape[-1]), p_wq[l], p_bq[l], p_keys[l], tables[l])
            lat = lat + m_lat[5] * ff.reshape(lat.shape)
        else:
            cx = cx + m_ctx[2] * yc
            hc2 = modulate(rms_norm(cx, g_ffn[l]), m_ctx[3], m_ctx[4])
            tokens = jnp.concatenate([hl2.reshape(n_lat, hl2.shape[-1]), hc2.reshape(-1, hc2.shape[-1])], axis=0)
            ff = peer(tokens, p_wq[l], p_bq[l], p_keys[l], tables[l])
            lat = lat + m_lat[5] * ff[:n_lat].reshape(lat.shape)
            cx = cx + m_ctx[5] * ff[n_lat:].reshape(cx.shape)
    return lat
```

```python
import functools
import math

import jax
import jax.numpy as jnp
import numpy as np
from jax import lax
from jax.experimental import pallas as pl
from jax.experimental.pallas import tpu as pltpu
from jax.experimental.pallas import tpu_sc as plsc

D_MODEL = 1024
DEPTH = 2
GRID_W = 64
EPS = 1e-6
N_MOD = 6

HG_HEADS = 4
HG_DK = 128
HG_DV = 128
HG_WK = HG_HEADS * HG_DK
HG_WV = HG_HEADS * HG_DV
HG_CHUNK = 64
RG_W = 512
RG_BLOCKS = 8
RG_BW = RG_W // RG_BLOCKS
RG_C = 8.0
CONV_W = 4
EVEN_SPLITS = (HG_WK, HG_WK + HG_WV, 2 * HG_WK + HG_WV, 3 * HG_WK + HG_WV, 3 * HG_WK + 2 * HG_WV,
               3 * HG_WK + 2 * HG_WV + RG_W)

ATT_HQ = 12
ATT_HKV = 4
ATT_G = ATT_HQ // ATT_HKV
ATT_DH = 64
ATT_WQ = ATT_HQ * ATT_DH
ATT_WKV = ATT_HKV * ATT_DH
WINDOW = 128
ATT_BLOCK = 128
ROPE_BASE = 10000.0
FN_GROUPS = 4
FN_DIM = 64
FN_W = FN_GROUPS * FN_DIM
ODD_SPLITS = (ATT_WQ, ATT_WQ + ATT_WKV, ATT_WQ + 2 * ATT_WKV)

PK_HEADS = 8
PK_NKEYS = 128
PK_DK = 256
PK_DKH = PK_DK // 2
PK_TOPK = 16
PK_CHUNK = 128


def _mm_kernel(a_ref, b_ref, o_ref):
    o_ref[...] = jnp.dot(a_ref[...].astype(jnp.bfloat16), b_ref[...].astype(jnp.bfloat16),
                         preferred_element_type=jnp.float32)


def pmm(a, b, tm=512, tn=512):
    m, k = a.shape
    n = b.shape[1]
    tm = next(t for t in (tm, 256, 128, m) if m % t == 0)
    tn = next(t for t in (tn, 256, 128, n) if n % t == 0)
    return pl.pallas_call(
        _mm_kernel,
        out_shape=jax.ShapeDtypeStruct((m, n), jnp.float32),
        grid=(m // tm, n // tn),
        in_specs=[pl.BlockSpec((tm, k), lambda i, j: (i, 0)),
                  pl.BlockSpec((k, tn), lambda i, j: (0, j))],
        out_specs=pl.BlockSpec((tm, tn), lambda i, j: (i, j)),
        compiler_params=pltpu.CompilerParams(dimension_semantics=("parallel", "parallel")),
        name="pmm",
    )(a, b)


def pmm3(a, b):
    bsz, t, k = a.shape
    return pmm(a.reshape(bsz * t, k), b).reshape(bsz, t, b.shape[1])


def rms_norm(x, g):
    xf = x.astype(jnp.float32)
    y = xf * lax.rsqrt(jnp.mean(xf * xf, axis=-1, keepdims=True) + EPS)
    return (y * g.astype(jnp.float32)).astype(x.dtype)


def modulate(h, shift, scale):
    return h * (1.0 + scale) + shift


def centred_dwconv(x, w, b):
    left = CONV_W // 2
    y = lax.conv_general_dilated(x, w.astype(x.dtype)[:, None, :], (1,), [(left, CONV_W - 1 - left)],
                                 dimension_numbers=('NWC', 'WIO', 'NWC'), feature_group_count=x.shape[-1])
    return y + b.astype(x.dtype)


HG_SUB = 16
NT_DIMS = (((1,), (1,)), ((), ()))
TN_DIMS = (((0,), (0,)), ((), ()))
HG_QCOL, HG_ICOL, HG_FCOL = 0, HG_HEADS, 2 * HG_HEADS


def _split3(x):
    hi = x.astype(jnp.bfloat16)
    r = x - hi.astype(jnp.float32)
    mid = r.astype(jnp.bfloat16)
    lo = (r - mid.astype(jnp.float32)).astype(jnp.bfloat16)
    return hi, mid, lo


def _dot01(m01, x):
    return sum(jnp.dot(m01, p, preferred_element_type=jnp.float32) for p in _split3(x))


def _hgrn_kernel(qr_ref, ir_ref, fr_ref, lb_ref, s0_ref, o_ref, send_ref, st_scr, *, reverse, n_chunks):
    c = pl.program_id(2)

    @pl.when(c == 0)
    def _():
        st_scr[...] = s0_ref[0, 0]

    n = HG_CHUNK
    row = lax.broadcasted_iota(jnp.int32, (n, n), 0)
    col = lax.broadcasted_iota(jnp.int32, (n, n), 1)
    lb = lb_ref[...]
    f = lb + (1.0 - lb) * jax.nn.sigmoid(fr_ref[0])
    k = 1.0 - f
    g = jnp.log(f)
    q = jax.nn.silu(qr_ref[0])
    v = ir_ref[0]
    if reverse:
        flipm = (row + col == n - 1).astype(jnp.bfloat16)
        q, k, v, g = (_dot01(flipm, a) for a in (q, k, v, g))
    b = _dot01((col <= row).astype(jnp.bfloat16), g)
    st = st_scr[...]
    o = lax.dot_general((q * jnp.exp(b)).astype(jnp.bfloat16), st.astype(jnp.bfloat16), NT_DIMS,
                        preferred_element_type=jnp.float32)
    srow = lax.broadcasted_iota(jnp.int32, (HG_SUB, HG_DK), 0)
    blocks = []
    for r0 in range(0, n, HG_SUB):
        b_i, q_i, k_i, v_i = (a[r0:r0 + HG_SUB] for a in (b, q, k, v))
        o_i = o[r0:r0 + HG_SUB]
        if r0 > 0:
            ref = b[r0 - 1:r0]
            a_q = (q_i * jnp.exp(b_i - ref)).astype(jnp.bfloat16)
            a_k = (k[0:r0] * jnp.exp(ref - b[0:r0])).astype(jnp.bfloat16)
            att = lax.dot_general(a_q, a_k, NT_DIMS, preferred_element_type=jnp.float32)
            o_i = o_i + jnp.dot(att.astype(jnp.bfloat16), v[0:r0].astype(jnp.bfloat16),
                                preferred_element_type=jnp.float32)
        diag = jnp.zeros((HG_SUB, HG_DV), jnp.float32)
        for t in range(HG_SUB):
            e = jnp.exp(jnp.minimum(b_i[t:t + 1] - b_i, 0.0))
            w = jnp.sum(jnp.where(srow <= t, q_i[t:t + 1] * k_i * e, 0.0), axis=1, keepdims=True)
            diag = jnp.where(srow == t, jnp.sum(w * v_i, axis=0, keepdims=True), diag)
        blocks.append(o_i + diag)
    o_full = jnp.concatenate(blocks, axis=0)
    if reverse:
        o_full = _dot01(flipm, o_full)
    o_ref[0] = o_full
    b_end = b[n - 1:n]
    k_dec = (k * jnp.exp(b_end - b)).astype(jnp.bfloat16)
    st_new = st * jnp.exp(b_end) + lax.dot_general(v.astype(jnp.bfloat16), k_dec, TN_DIMS,
                                                   preferred_element_type=jnp.float32)
    st_scr[...] = st_new

    @pl.when(c == n_chunks - 1)
    def _():
        send_ref[0, 0] = st_new


def hgrn_scan(proj, lb, s0t, reverse):
    bsz, t_len, _ = proj.shape
    n_chunks = t_len // HG_CHUNK
    fcol = HG_FCOL + (HG_HEADS if reverse else 0)

    def tmap(c):
        return n_chunks - 1 - c if reverse else c

    blk = (1, HG_CHUNK, HG_DK)
    return pl.pallas_call(
        functools.partial(_hgrn_kernel, reverse=reverse, n_chunks=n_chunks),
        out_shape=(jax.ShapeDtypeStruct((bsz, t_len, HG_HEADS * HG_DV), jnp.float32),
                   jax.ShapeDtypeStruct((bsz, HG_HEADS, HG_DV, HG_DK), jnp.float32)),
        grid=(bsz, HG_HEADS, n_chunks),
        in_specs=[pl.BlockSpec(blk, lambda b, h, c: (b, tmap(c), HG_QCOL + h)),
                  pl.BlockSpec(blk, lambda b, h, c: (b, tmap(c), HG_ICOL + h)),
                  pl.BlockSpec(blk, lambda b, h, c: (b, tmap(c), fcol + h)),
                  pl.BlockSpec((1, HG_DK), lambda b, h, c: (0, h)),
                  pl.BlockSpec((1, 1, HG_DV, HG_DK), lambda b, h, c: (b, h, 0, 0))],
        out_specs=(pl.BlockSpec(blk, lambda b, h, c: (b, tmap(c), h)),
                   pl.BlockSpec((1, 1, HG_DV, HG_DK), lambda b, h, c: (b, h, 0, 0))),
        scratch_shapes=[pltpu.VMEM((HG_DV, HG_DK), jnp.float32)],
        compiler_params=pltpu.CompilerParams(dimension_semantics=("parallel", "parallel", "arbitrary")),
        name="hgrn_scan",
    )(proj, proj, proj, lb, s0t)


def block_diag(x, w):
    bsz, t_len, _ = x.shape
    y = jnp.einsum('btnc,ncd->btnd', x.reshape(bsz, t_len, RG_BLOCKS, RG_BW), w)
    return y.reshape(bsz, t_len, RG_W)


def rglru_gates(xc, wa, ba, wx, bx, lam):
    r = jax.nn.sigmoid(block_diag(xc, wa) + ba)
    i = jax.nn.sigmoid(block_diag(xc, wx) + bx)
    log_a = -RG_C * r * jax.nn.softplus(-lam.astype(jnp.float32))
    u = jnp.sqrt(-jnp.expm1(2.0 * log_a)) * (i * xc)
    return log_a, u


RG_TB = 256
RG_SUB = 16


def _rglru_kernel(la_ref, u_ref, h0_ref, h_ref, hend_ref, h_scr, *, reverse, n_blocks):
    c = pl.program_id(1)

    @pl.when(c == 0)
    def _():
        h_scr[...] = h0_ref[0]

    row = lax.broadcasted_iota(jnp.int32, (RG_SUB, RG_SUB), 0)
    col = lax.broadcasted_iota(jnp.int32, (RG_SUB, RG_SUB), 1)
    tri = ((col >= row) if reverse else (col <= row)).astype(jnp.bfloat16)
    srow = lax.broadcasted_iota(jnp.int32, (RG_SUB, RG_W), 0)
    n_sub = h_ref.shape[1] // RG_SUB

    def sub_block(j, h_prev):
        jj = n_sub - 1 - j if reverse else j
        r0 = pl.multiple_of(jj * RG_SUB, RG_SUB)
        la = la_ref[0, pl.ds(r0, RG_SUB), :]
        uu = u_ref[0, pl.ds(r0, RG_SUB), :]
        acc_la = _dot01(tri, la)
        out = jnp.zeros((RG_SUB, RG_W), jnp.float32)
        for t in range(RG_SUB):
            seen = (srow >= t) if reverse else (srow <= t)
            e = jnp.exp(jnp.minimum(acc_la[t:t + 1] - acc_la, 0.0))
            h_t = jnp.sum(jnp.where(seen, e * uu, 0.0), axis=0, keepdims=True) + jnp.exp(acc_la[t:t + 1]) * h_prev
            out = jnp.where(srow == t, h_t, out)
        h_ref[0, pl.ds(r0, RG_SUB), :] = out
        last = 0 if reverse else RG_SUB - 1
        return out[last:last + 1]

    h_last = lax.fori_loop(0, n_sub, sub_block, h_scr[...])
    h_scr[...] = h_last

    @pl.when(c == n_blocks - 1)
    def _():
        hend_ref[0] = h_last


def rglru_scan(log_a, u, h0, reverse):
    bsz, t_len, w = log_a.shape
    tb = min(RG_TB, t_len)
    n_blocks = t_len // tb

    def tmap(c):
        return n_blocks - 1 - c if reverse else c

    blk = (1, tb, w)
    return pl.pallas_call(
        functools.partial(_rglru_kernel, reverse=reverse, n_blocks=n_blocks),
        out_shape=(jax.ShapeDtypeStruct((bsz, t_len, w), jnp.float32),
                   jax.ShapeDtypeStruct((bsz, 1, w), jnp.float32)),
        grid=(bsz, n_blocks),
        in_specs=[pl.BlockSpec(blk, lambda b, c: (b, tmap(c), 0)),
                  pl.BlockSpec(blk, lambda b, c: (b, tmap(c), 0)),
                  pl.BlockSpec((1, 1, w), lambda b, c: (b, 0, 0))],
        out_specs=(pl.BlockSpec(blk, lambda b, c: (b, tmap(c), 0)),
                   pl.BlockSpec((1, 1, w), lambda b, c: (b, 0, 0))),
        scratch_shapes=[pltpu.VMEM((1, w), jnp.float32)],
        compiler_params=pltpu.CompilerParams(dimension_semantics=("parallel", "arbitrary")),
        name="rglru_scan",
    )(log_a, u, h0)


def rglru_bidir(g_f, g_b, h_f, h_b):
    y_f, h_f = rglru_scan(g_f[0], g_f[1], h_f, False)
    y_b, h_b = rglru_scan(g_b[0], g_b[1], h_b, True)
    return y_f + y_b, h_f, h_b


def even_mixer(hl, hc, w_in, w_out, lb, hg_gain, conv_w, conv_b, wa, ba, wx, bx, lam):
    def prepare(h):
        proj = pmm3(h, w_in)
        _, _, _, _, g, xr, gate = jnp.split(proj, EVEN_SPLITS, axis=-1)
        xc = centred_dwconv(xr, conv_w, conv_b).astype(jnp.float32)
        g_f = rglru_gates(xc, wa[0], ba[0], wx[0], bx[0], lam[0])
        g_b = rglru_gates(xc, wa[1], ba[1], wx[1], bx[1], lam[1])
        return proj, (g_f, g_b), (g, gate)

    def hgrn_bidir(proj, s_f, s_b):
        o_f, s_f = hgrn_scan(proj, lb[0:1], s_f, False)
        o_b, s_b = hgrn_scan(proj, lb[1:2], s_b, True)
        return o_f + o_b, s_f, s_b

    def merge(o_hg, h_rg, g, gate):
        bsz, t_len, _ = g.shape
        o_hg = o_hg.reshape(bsz, t_len, HG_HEADS, HG_DV)
        y_a = rms_norm(o_hg, hg_gain) * jax.nn.silu(g.astype(jnp.float32)).reshape(bsz, t_len, HG_HEADS, HG_DV)
        y_b = jax.nn.gelu(gate.astype(jnp.float32)) * h_rg
        y = jnp.concatenate([y_a.reshape(bsz, t_len, HG_WV), y_b], axis=-1)
        return pmm3(y.astype(g.dtype), w_out)

    hg_c, rg_c, gt_c = prepare(hc)
    hg_l, rg_l, gt_l = prepare(hl)
    bsz = hl.shape[0]
    s0 = jnp.zeros((bsz, HG_HEADS, HG_DK, HG_DV), jnp.float32)
    h0 = jnp.zeros((bsz, 1, RG_W), jnp.float32)
    o_c, s_f, s_b = hgrn_bidir(hg_c, s0, s0)
    r_c, h_f, h_b = rglru_bidir(*rg_c, h0, h0)
    o_l, _, _ = hgrn_bidir(hg_l, s_f, s_b)
    r_l, _, _ = rglru_bidir(*rg_l, h_f, h_b)
    return merge(o_l, r_l, *gt_l), merge(o_c, r_c, *gt_c)


def axial_rope(t_len):
    rows = t_len // GRID_W
    row = jnp.repeat(jnp.arange(rows, dtype=jnp.float32), GRID_W)
    col = jnp.tile(jnp.arange(GRID_W, dtype=jnp.float32), rows)
    n_freq = ATT_DH // 4
    inv = ROPE_BASE ** (-jnp.arange(n_freq, dtype=jnp.float32) / n_freq)
    ang = jnp.concatenate([row[:, None] * inv, col[:, None] * inv], axis=-1)
    return jnp.cos(ang), jnp.sin(ang)


def apply_rope(x, cos, sin):
    xf = x.astype(jnp.float32)
    x1, x2 = jnp.split(xf, 2, axis=-1)
    cs, sn = cos[None, :, None, :], sin[None, :, None, :]
    return jnp.concatenate([x1 * cs - x2 * sn, x2 * cs + x1 * sn], axis=-1).astype(x.dtype)


def sink_attention(q, k, v, sink, mask):
    s = jnp.einsum('bqhgd,bkhd->bhgqk', q, k).astype(jnp.float32) * (ATT_DH ** -0.5)
    if mask is not None:
        s = jnp.where(mask, s, -jnp.inf)
    sk = jnp.broadcast_to(sink.astype(jnp.float32)[None, :, :, None, None], s.shape[:-1] + (1,))
    p = jax.nn.softmax(jnp.concatenate([sk, s], axis=-1), axis=-1)[..., 1:]
    return jnp.einsum('bhgqk,bkhd->bqhgd', p.astype(v.dtype), v)


def windowed_attention(q, k, v, kc, vc, sink):
    bsz, t_len = q.shape[:2]
    n_blk = t_len // ATT_BLOCK
    n_ctx = kc.shape[1]
    pad = ((0, 0), (ATT_BLOCK, ATT_BLOCK), (0, 0), (0, 0))
    kp, vp = jnp.pad(k, pad), jnp.pad(v, pad)
    qb = q.reshape(bsz, n_blk, ATT_BLOCK, ATT_HKV, ATT_G, ATT_DH).transpose(1, 0, 2, 3, 4, 5)
    offs_q = jnp.arange(ATT_BLOCK)
    offs_k = jnp.arange(3 * ATT_BLOCK)
    ctx_mask = jnp.ones((ATT_BLOCK, n_ctx), dtype=bool)

    def block(args):
        n, qn = args
        kn = lax.dynamic_slice_in_dim(kp, n * ATT_BLOCK, 3 * ATT_BLOCK, axis=1)
        vn = lax.dynamic_slice_in_dim(vp, n * ATT_BLOCK, 3 * ATT_BLOCK, axis=1)
        qpos = n * ATT_BLOCK + offs_q
        kpos = (n - 1) * ATT_BLOCK + offs_k
        local = (jnp.abs(qpos[:, None] - kpos[None, :]) <= WINDOW) & (kpos >= 0)[None, :] & (kpos < t_len)[None, :]
        mask = jnp.concatenate([ctx_mask, local], axis=1)
        return sink_attention(qn, jnp.concatenate([kc, kn], axis=1), jnp.concatenate([vc, vn], axis=1), sink, mask)

    o = lax.map(block, (jnp.arange(n_blk), qb))
    return o.transpose(1, 0, 2, 3, 4, 5).reshape(bsz, t_len, ATT_WQ)


def fourier_mix(z):
    bsz, t_len, _ = z.shape
    zz = z.astype(jnp.float32).reshape(bsz, t_len, FN_GROUPS, FN_DIM)
    y = jnp.fft.fft2(zz, axes=(1, 3), norm='ortho').real
    return y.reshape(bsz, t_len, FN_W).astype(z.dtype)


def odd_mixer(hl, hc, w_in, w_out, q_gain, k_gain, sink, need_ctx):
    bsz, t_len, _ = hl.shape
    n_ctx = hc.shape[1]
    ql, kl, vl, fl = jnp.split(pmm3(hl, w_in), ODD_SPLITS, axis=-1)
    cos, sin = axial_rope(t_len)
    ql = apply_rope(rms_norm(ql.reshape(bsz, t_len, ATT_HQ, ATT_DH), q_gain), cos, sin)
    kl = apply_rope(rms_norm(kl.reshape(bsz, t_len, ATT_HKV, ATT_DH), k_gain), cos, sin)
    vl = vl.reshape(bsz, t_len, ATT_HKV, ATT_DH)
    kc, vc = jnp.split(pmm3(hc, w_in[:, ATT_WQ:ATT_WQ + 2 * ATT_WKV]), 2, axis=-1)
    kc = rms_norm(kc.reshape(bsz, n_ctx, ATT_HKV, ATT_DH), k_gain)
    vc = vc.reshape(bsz, n_ctx, ATT_HKV, ATT_DH)
    sink_g = sink.reshape(ATT_HKV, ATT_G)
    a_l = windowed_attention(ql.reshape(bsz, t_len, ATT_HKV, ATT_G, ATT_DH), kl, vl, kc, vc, sink_g)
    y_l = pmm3(jnp.concatenate([a_l, fourier_mix(fl)], axis=-1), w_out)
    if not need_ctx:
        return y_l, None
    qc = rms_norm(pmm3(hc, w_in[:, :ATT_WQ]).reshape(bsz, n_ctx, ATT_HKV, ATT_G, ATT_DH), q_gain)
    fc = pmm3(hc, w_in[:, ATT_WQ + 2 * ATT_WKV:])
    a_c = sink_attention(qc, kc, vc, sink_g, None).reshape(bsz, n_ctx, ATT_WQ)
    y_c = pmm3(jnp.concatenate([a_c, fourier_mix(fc)], axis=-1), w_out)
    return y_l, y_c


PEER_TA = 256
PEER_TG = 8
PEER_GPB = 2
PEER_SUB = 8
PEER_COLS = PK_HEADS * PK_TOPK * PEER_SUB
PEER_CHUNK = 4096
SC_W = 32
SC_IDX_BLOCK = 128


def _lane_group_allsum(y, lane):
    k = 1
    while k < PEER_SUB:
        up = pltpu.roll(y, 128 - k, axis=1)
        dn = pltpu.roll(y, k, axis=1)
        y = y + jnp.where((lane & k) == 0, up, dn)
        k *= 2
    return y


def _mix_kernel(x2_ref, gate_ref, g_ref, o2_ref, hp_ref):
    lane = lax.broadcasted_iota(jnp.int32, (PEER_SUB, PEER_COLS), 1)
    sub = lax.broadcasted_iota(jnp.int32, (PEER_SUB, PEER_COLS), 0)
    same_s = (lane % PEER_SUB) == sub
    lane128 = lax.broadcasted_iota(jnp.int32, (PEER_TG, 128), 1)
    for g in range(PEER_GPB):
        for t in range(PEER_TG):
            tok = g * PEER_TG + t
            w = g_ref[pl.ds(tok * PEER_COLS, PEER_COLS), :]
            ub = lax.bitcast_convert_type(w << 16, jnp.float32).astype(jnp.bfloat16)
            x2 = x2_ref[pl.ds(tok * PEER_SUB, PEER_SUB), :].astype(jnp.bfloat16)
            p = lax.dot_general(x2, ub, NT_DIMS, preferred_element_type=jnp.float32)
            hp_ref[pl.ds(t, 1), :] = jnp.sum(jnp.where(same_s, p, 0.0), axis=0, keepdims=True)
        hp = hp_ref[...]
        hfull = jnp.concatenate(
            [_lane_group_allsum(hp[:, c * 128:(c + 1) * 128], lane128) for c in range(PEER_COLS // 128)], axis=1)
        act = jax.nn.gelu(hfull) * gate_ref[pl.ds(g * PEER_TG, PEER_TG), :]
        for t in range(PEER_TG):
            tok = g * PEER_TG + t
            a2 = jnp.where(same_s, act[t:t + 1, :], 0.0).astype(jnp.bfloat16)
            w = g_ref[pl.ds(tok * PEER_COLS, PEER_COLS), :]
            vb = lax.bitcast_convert_type(w & jnp.int32(-65536), jnp.float32).astype(jnp.bfloat16)
            o2_ref[pl.ds(tok * PEER_SUB, PEER_SUB), :] = jnp.dot(a2, vb, preferred_element_type=jnp.float32)


def peer_mix(x, gate, gathered):
    n, d = x.shape
    tb = PEER_TG * PEER_GPB
    x2 = x.reshape(n * PEER_SUB, 128)
    gate8 = jnp.repeat(gate, PEER_SUB, axis=1)
    g2 = gathered.reshape(n * PEER_COLS, 128)
    out2 = pl.pallas_call(
        _mix_kernel,
        out_shape=jax.ShapeDtypeStruct((n * PEER_SUB, 128), jnp.float32),
        grid=(n // tb,),
        in_specs=[pl.BlockSpec((tb * PEER_SUB, 128), lambda i: (i, 0)),
                  pl.BlockSpec((tb, PEER_COLS), lambda i: (i, 0)),
                  pl.BlockSpec((tb * PEER_COLS, 128), lambda i: (i, 0))],
        out_specs=pl.BlockSpec((tb * PEER_SUB, 128), lambda i: (i, 0)),
        scratch_shapes=[pltpu.VMEM((PEER_TG, PEER_COLS), jnp.float32)],
        compiler_params=pltpu.CompilerParams(dimension_semantics=("parallel",),
                                             vmem_limit_bytes=48 * 1024 * 1024),
        name="peer_mix",
    )(x2, gate8, g2)
    return out2.reshape(n, d)


def sc_gather(table, idxp):
    nblk = idxp.shape[1] // SC_IDX_BLOCK
    mesh = plsc.VectorSubcoreMesh(core_axis_name="c", subcore_axis_name="s")

    @functools.partial(pl.kernel, out_type=jax.ShapeDtypeStruct((nblk * SC_W, 8, 128), table.dtype), mesh=mesh)
    def gather_kernel(t_hbm, i_hbm, o_hbm):
        def body(i_vmem, o_vmem):
            pltpu.sync_copy(t_hbm.at[i_vmem.at[0, pl.ds(0, SC_W)]], o_vmem)

        pltpu.emit_pipeline(
            body, grid=(nblk,),
            in_specs=[pl.BlockSpec((1, SC_IDX_BLOCK), lambda i: (0, i))],
            out_specs=[pl.BlockSpec((SC_W, 8, 128), lambda i: (i, 0, 0))],
            core_axis_name=("c", "s"),
            dimension_semantics=(pltpu.PARALLEL,),
            trace_scopes=False,
        )(i_hbm, o_hbm)

    return gather_kernel(table, idxp)


def pack_table(u, v):
    ub = lax.bitcast_convert_type(u.astype(jnp.bfloat16), jnp.uint16).astype(jnp.uint32)
    vb = lax.bitcast_convert_type(v.astype(jnp.bfloat16), jnp.uint16).astype(jnp.uint32)
    w = lax.bitcast_convert_type(ub | (vb << 16), jnp.int32)
    return w.reshape(u.shape[0], 8, 128)


def topk_last(s, k):
    n = s.shape[-1]
    iota = lax.broadcasted_iota(jnp.int32, s.shape, s.ndim - 1)
    vals, idxs = [], []
    for _ in range(k):
        m = jnp.max(s, axis=-1, keepdims=True)
        am = jnp.min(jnp.where(s == m, iota, n), axis=-1, keepdims=True)
        vals.append(m)
        idxs.append(am)
        s = jnp.where(iota == am, -jnp.inf, s)
    return jnp.concatenate(vals, axis=-1), jnp.concatenate(idxs, axis=-1)


def peer(h, w_q, b_q, sub_keys, table):
    n, d = h.shape
    qq = (pmm(h, w_q) + b_q).reshape(n, PK_HEADS, 2, PK_DKH)
    ss = jnp.einsum('thpd,hpkd->thpk', qq, sub_keys.astype(jnp.float32))
    sv, si = topk_last(ss, PK_TOPK)
    cand = (sv[:, :, 0, :, None] + sv[:, :, 1, None, :]).reshape(n, PK_HEADS, PK_TOPK * PK_TOPK)
    cidx = (si[:, :, 0, :, None] * PK_NKEYS + si[:, :, 1, None, :]).reshape(n, PK_HEADS, PK_TOPK * PK_TOPK)
    best, pos = topk_last(cand, PK_TOPK)
    eidx = jnp.take_along_axis(cidx, pos, axis=-1).reshape(n, PK_HEADS * PK_TOPK)
    gate = jax.nn.softmax(best, axis=-1).reshape(n, PK_HEADS * PK_TOPK)
    outs = []
    for c0 in range(0, n, PEER_CHUNK):
        c1 = min(n, c0 + PEER_CHUNK)
        e = eidx[c0:c1].reshape(-1, SC_W)
        idxp = jnp.pad(e, ((0, 0), (0, SC_IDX_BLOCK - SC_W))).reshape(1, -1)
        gathered = sc_gather(table, idxp)
        outs.append(peer_mix(h[c0:c1], gate[c0:c1], gathered))
    return jnp.concatenate(outs, axis=0)


def kernel(x, c, ctx, c_ctx, w_mod, b_mod, g_mix, g_ffn, e_w_in, e_w_out, hg_lb_logits, hg_gain, rg_conv_w,
           rg_conv_b, rg_wa, rg_ba, rg_wx, rg_bx, rg_lambda, o_w_in, o_w_out, q_gain, k_gain, sinks, p_wq,
           p_bq, p_keys, p_u, p_v):
    lb_all = jnp.cumsum(jax.nn.softmax(hg_lb_logits.astype(jnp.float32), axis=1), axis=1)
    tables = [pack_table(p_u[l], p_v[l]) for l in range(DEPTH)]
    lat, cx = x, ctx
    for l in range(DEPTH):
        last = l == DEPTH - 1
        j = l // 2
        m_lat = jnp.split((jax.nn.silu(c) @ w_mod[l] + b_mod[l])[:, None, :], N_MOD, axis=-1)
        m_ctx = jnp.split(jax.nn.silu(c_ctx) @ w_mod[l] + b_mod[l], N_MOD, axis=-1)
        hl = modulate(rms_norm(lat, g_mix[l]), m_lat[0], m_lat[1])
        hc = modulate(rms_norm(cx, g_mix[l]), m_ctx[0], m_ctx[1])
        if l % 2 == 0:
            yl, yc = even_mixer(hl, hc, e_w_in[j], e_w_out[j], lb_all[:, j], hg_gain[j], rg_conv_w[j], rg_conv_b[j],
                                rg_wa[j], rg_ba[j], rg_wx[j], rg_bx[j], rg_lambda[j])
        else:
            yl, yc = odd_mixer(hl, hc, o_w_in[j], o_w_out[j], q_gain[j], k_gain[j], sinks[j], not last)
        lat = lat + m_lat[2] * yl
        hl2 = modulate(rms_norm(lat, g_ffn[l]), m_lat[3], m_lat[4])
        n_lat = hl2.shape[0] * hl2.shape[1]
        if last:
            ff = peer(hl2.reshape(n_lat, hl2.shape[-1]), p_wq[l], p_bq[l], p_keys[l], tables[l])
            lat = lat + m_lat[5] * ff.reshape(lat.shape)
        else:
            cx = cx + m_ctx[2] * yc
            hc2 = modulate(rms_norm(cx, g_ffn[l]), m_ctx[3], m_ctx[4])
            tokens = jnp.concatenate([hl2.reshape(n_lat, hl2.shape[-1]), hc2.reshape(-1, hc2.shape[-1])], axis=0)
            ff = peer(tokens, p_wq[l], p_bq[l], p_keys[l], tables[l])
            lat = lat + m_lat[5] * ff[:n_lat].reshape(lat.shape)
            cx = cx + m_ctx[5] * ff[n_lat:].reshape(cx.shape)
    return lat
```

```python
import functools
import math

import jax
import jax.numpy as jnp
import numpy as np
from jax import lax
from jax.experimental import pallas as pl
from jax.experimental.pallas import tpu as pltpu
from jax.experimental.pallas import tpu_sc as plsc

D_MODEL = 1024
DEPTH = 2
GRID_W = 64
EPS = 1e-6
N_MOD = 6

HG_HEADS = 4
HG_DK = 128
HG_DV = 128
HG_WK = HG_HEADS * HG_DK
HG_WV = HG_HEADS * HG_DV
HG_CHUNK = 64
RG_W = 512
RG_BLOCKS = 8
RG_BW = RG_W // RG_BLOCKS
RG_C = 8.0
CONV_W = 4
EVEN_SPLITS = (HG_WK, HG_WK + HG_WV, 2 * HG_WK + HG_WV, 3 * HG_WK + HG_WV, 3 * HG_WK + 2 * HG_WV,
               3 * HG_WK + 2 * HG_WV + RG_W)

ATT_HQ = 12
ATT_HKV = 4
ATT_G = ATT_HQ // ATT_HKV
ATT_DH = 64
ATT_WQ = ATT_HQ * ATT_DH
ATT_WKV = ATT_HKV * ATT_DH
WINDOW = 128
ATT_BLOCK = 128
ROPE_BASE = 10000.0
FN_GROUPS = 4
FN_DIM = 64
FN_W = FN_GROUPS * FN_DIM
ODD_SPLITS = (ATT_WQ, ATT_WQ + ATT_WKV, ATT_WQ + 2 * ATT_WKV)

PK_HEADS = 8
PK_NKEYS = 128
PK_DK = 256
PK_DKH = PK_DK // 2
PK_TOPK = 16
PK_CHUNK = 128


def _mm_kernel(a_ref, b_ref, o_ref):
    o_ref[...] = jnp.dot(a_ref[...].astype(jnp.bfloat16), b_ref[...].astype(jnp.bfloat16),
                         preferred_element_type=jnp.float32)


def pmm(a, b, tm=512, tn=512):
    m, k = a.shape
    n = b.shape[1]
    tm = next(t for t in (tm, 256, 128, m) if m % t == 0)
    tn = next(t for t in (tn, 256, 128, n) if n % t == 0)
    return pl.pallas_call(
        _mm_kernel,
        out_shape=jax.ShapeDtypeStruct((m, n), jnp.float32),
        grid=(m // tm, n // tn),
        in_specs=[pl.BlockSpec((tm, k), lambda i, j: (i, 0)),
                  pl.BlockSpec((k, tn), lambda i, j: (0, j))],
        out_specs=pl.BlockSpec((tm, tn), lambda i, j: (i, j)),
        compiler_params=pltpu.CompilerParams(dimension_semantics=("parallel", "parallel")),
        name="pmm",
    )(a, b)


def pmm3(a, b):
    bsz, t, k = a.shape
    return pmm(a.reshape(bsz * t, k), b).reshape(bsz, t, b.shape[1])


def rms_norm(x, g):
    xf = x.astype(jnp.float32)
    y = xf * lax.rsqrt(jnp.mean(xf * xf, axis=-1, keepdims=True) + EPS)
    return (y * g.astype(jnp.float32)).astype(x.dtype)


def modulate(h, shift, scale):
    return h * (1.0 + scale) + shift


def centred_dwconv(x, w, b):
    left = CONV_W // 2
    y = lax.conv_general_dilated(x, w.astype(x.dtype)[:, None, :], (1,), [(left, CONV_W - 1 - left)],
                                 dimension_numbers=('NWC', 'WIO', 'NWC'), feature_group_count=x.shape[-1])
    return y + b.astype(x.dtype)


HG_SUB = 16
NT_DIMS = (((1,), (1,)), ((), ()))
TN_DIMS = (((0,), (0,)), ((), ()))
HG_QCOL, HG_ICOL, HG_FCOL = 0, HG_HEADS, 2 * HG_HEADS


def _split3(x):
    hi = x.astype(jnp.bfloat16)
    r = x - hi.astype(jnp.float32)
    mid = r.astype(jnp.bfloat16)
    lo = (r - mid.astype(jnp.float32)).astype(jnp.bfloat16)
    return hi, mid, lo


def _dot01(m01, x):
    return sum(jnp.dot(m01, p, preferred_element_type=jnp.float32) for p in _split3(x))


def _hgrn_kernel(qr_ref, ir_ref, fr_ref, lb_ref, s0_ref, o_ref, send_ref, st_scr, *, reverse, n_chunks):
    c = pl.program_id(2)

    @pl.when(c == 0)
    def _():
        st_scr[...] = s0_ref[0, 0]

    n = HG_CHUNK
    row = lax.broadcasted_iota(jnp.int32, (n, n), 0)
    col = lax.broadcasted_iota(jnp.int32, (n, n), 1)
    lb = lb_ref[...]
    f = lb + (1.0 - lb) * jax.nn.sigmoid(fr_ref[0])
    k = 1.0 - f
    g = jnp.log(f)
    q = jax.nn.silu(qr_ref[0])
    v = ir_ref[0]
    if reverse:
        flipm = (row + col == n - 1).astype(jnp.bfloat16)
        q, k, v, g = (_dot01(flipm, a) for a in (q, k, v, g))
    b = _dot01((col <= row).astype(jnp.bfloat16), g)
    st = st_scr[...]
    o = lax.dot_general((q * jnp.exp(b)).astype(jnp.bfloat16), st.astype(jnp.bfloat16), NT_DIMS,
                        preferred_element_type=jnp.float32)
    srow = lax.broadcasted_iota(jnp.int32, (HG_SUB, HG_DK), 0)
    blocks = []
    for r0 in range(0, n, HG_SUB):
        b_i, q_i, k_i, v_i = (a[r0:r0 + HG_SUB] for a in (b, q, k, v))
        o_i = o[r0:r0 + HG_SUB]
        if r0 > 0:
            ref = b[r0 - 1:r0]
            a_q = (q_i * jnp.exp(b_i - ref)).astype(jnp.bfloat16)
            a_k = (k[0:r0] * jnp.exp(ref - b[0:r0])).astype(jnp.bfloat16)
            att = lax.dot_general(a_q, a_k, NT_DIMS, preferred_element_type=jnp.float32)
            o_i = o_i + jnp.dot(att.astype(jnp.bfloat16), v[0:r0].astype(jnp.bfloat16),
                                preferred_element_type=jnp.float32)
        diag = jnp.zeros((HG_SUB, HG_DV), jnp.float32)
        for t in range(HG_SUB):
            e = jnp.exp(jnp.minimum(b_i[t:t + 1] - b_i, 0.0))
            w = jnp.sum(jnp.where(srow <= t, q_i[t:t + 1] * k_i * e, 0.0), axis=1, keepdims=True)
            diag = jnp.where(srow == t, jnp.sum(w * v_i, axis=0, keepdims=True), diag)
        blocks.append(o_i + diag)
    o_full = jnp.concatenate(blocks, axis=0)
    if reverse:
        o_full = _dot01(flipm, o_full)
    o_ref[0] = o_full
    b_end = b[n - 1:n]
    k_dec = (k * jnp.exp(b_end - b)).astype(jnp.bfloat16)
    st_new = st * jnp.exp(b_end) + lax.dot_general(v.astype(jnp.bfloat16), k_dec, TN_DIMS,
                                                   preferred_element_type=jnp.float32)
    st_scr[...] = st_new

    @pl.when(c == n_chunks - 1)
    def _():
        send_ref[0, 0] = st_new


def hgrn_scan(proj, lb, s0t, reverse):
    bsz, t_len, _ = proj.shape
    n_chunks = t_len // HG_CHUNK
    fcol = HG_FCOL + (HG_HEADS if reverse else 0)

    def tmap(c):
        return n_chunks - 1 - c if reverse else c

    blk = (1, HG_CHUNK, HG_DK)
    return pl.pallas_call(
        functools.partial(_hgrn_kernel, reverse=reverse, n_chunks=n_chunks),
        out_shape=(jax.ShapeDtypeStruct((bsz, t_len, HG_HEADS * HG_DV), jnp.float32),
                   jax.ShapeDtypeStruct((bsz, HG_HEADS, HG_DV, HG_DK), jnp.float32)),
        grid=(bsz, HG_HEADS, n_chunks),
        in_specs=[pl.BlockSpec(blk, lambda b, h, c: (b, tmap(c), HG_QCOL + h)),
                  pl.BlockSpec(blk, lambda b, h, c: (b, tmap(c), HG_ICOL + h)),
                  pl.BlockSpec(blk, lambda b, h, c: (b, tmap(c), fcol + h)),
                  pl.BlockSpec((1, HG_DK), lambda b, h, c: (0, h)),
                  pl.BlockSpec((1, 1, HG_DV, HG_DK), lambda b, h, c: (b, h, 0, 0))],
        out_specs=(pl.BlockSpec(blk, lambda b, h, c: (b, tmap(c), h)),
                   pl.BlockSpec((1, 1, HG_DV, HG_DK), lambda b, h, c: (b, h, 0, 0))),
        scratch_shapes=[pltpu.VMEM((HG_DV, HG_DK), jnp.float32)],
        compiler_params=pltpu.CompilerParams(dimension_semantics=("parallel", "parallel", "arbitrary")),
        name="hgrn_scan",
    )(proj, proj, proj, lb, s0t)


def block_diag(x, w):
    bsz, t_len, _ = x.shape
    y = jnp.einsum('btnc,ncd->btnd', x.reshape(bsz, t_len, RG_BLOCKS, RG_BW), w)
    return y.reshape(bsz, t_len, RG_W)


def rglru_gates(xc, wa, ba, wx, bx, lam):
    r = jax.nn.sigmoid(block_diag(xc, wa) + ba)
    i = jax.nn.sigmoid(block_diag(xc, wx) + bx)
    log_a = -RG_C * r * jax.nn.softplus(-lam.astype(jnp.float32))
    u = jnp.sqrt(-jnp.expm1(2.0 * log_a)) * (i * xc)
    return log_a, u


RG_TB = 256
RG_SUB = 16


def _rglru_kernel(la_ref, u_ref, h0_ref, h_ref, hend_ref, h_scr, *, reverse, n_blocks):
    c = pl.program_id(1)

    @pl.when(c == 0)
    def _():
        h_scr[...] = h0_ref[0]

    row = lax.broadcasted_iota(jnp.int32, (RG_SUB, RG_SUB), 0)
    col = lax.broadcasted_iota(jnp.int32, (RG_SUB, RG_SUB), 1)
    tri = ((col >= row) if reverse else (col <= row)).astype(jnp.bfloat16)
    srow = lax.broadcasted_iota(jnp.int32, (RG_SUB, RG_W), 0)
    n_sub = h_ref.shape[1] // RG_SUB

    def sub_block(j, h_prev):
        jj = n_sub - 1 - j if reverse else j
        r0 = pl.multiple_of(jj * RG_SUB, RG_SUB)
        la = la_ref[0, pl.ds(r0, RG_SUB), :]
        uu = u_ref[0, pl.ds(r0, RG_SUB), :]
        acc_la = _dot01(tri, la)
        out = jnp.zeros((RG_SUB, RG_W), jnp.float32)
        for t in range(RG_SUB):
            seen = (srow >= t) if reverse else (srow <= t)
            e = jnp.exp(jnp.minimum(acc_la[t:t + 1] - acc_la, 0.0))
            h_t = jnp.sum(jnp.where(seen, e * uu, 0.0), axis=0, keepdims=True) + jnp.exp(acc_la[t:t + 1]) * h_prev
            out = jnp.where(srow == t, h_t, out)
        h_ref[0, pl.ds(r0, RG_SUB), :] = out
        last = 0 if reverse else RG_SUB - 1
        return out[last:last + 1]

    h_last = lax.fori_loop(0, n_sub, sub_block, h_scr[...])
    h_scr[...] = h_last

    @pl.when(c == n_blocks - 1)
    def _():
        hend_ref[0] = h_last


def rglru_scan(log_a, u, h0, reverse):
    bsz, t_len, w = log_a.shape
    tb = min(RG_TB, t_len)
    n_blocks = t_len // tb

    def tmap(c):
        return n_blocks - 1 - c if reverse else c

    blk = (1, tb, w)
    return pl.pallas_call(
        functools.partial(_rglru_kernel, reverse=reverse, n_blocks=n_blocks),
        out_shape=(jax.ShapeDtypeStruct((bsz, t_len, w), jnp.float32),
                   jax.ShapeDtypeStruct((bsz, 1, w), jnp.float32)),
        grid=(bsz, n_blocks),
        in_specs=[pl.BlockSpec(blk, lambda b, c: (b, tmap(c), 0)),
                  pl.BlockSpec(blk, lambda b, c: (b, tmap(c), 0)),
                  pl.BlockSpec((1, 1, w), lambda b, c: (b, 0, 0))],
        out_specs=(pl.BlockSpec(blk, lambda b, c: (b, tmap(c), 0)),
                   pl.BlockSpec((1, 1, w), lambda b, c: (b, 0, 0))),
        scratch_shapes=[pltpu.VMEM((1, w), jnp.float32)],
        compiler_params=pltpu.CompilerParams(dimension_semantics=("parallel", "arbitrary")),
        name="rglru_scan",
    )(log_a, u, h0)


def rglru_bidir(g_f, g_b, h_f, h_b):
    y_f, h_f = rglru_scan(g_f[0], g_f[1], h_f, False)
    y_b, h_b = rglru_scan(g_b[0], g_b[1], h_b, True)
    return y_f + y_b, h_f, h_b


def even_mixer(hl, hc, w_in, w_out, lb, hg_gain, conv_w, conv_b, wa, ba, wx, bx, lam):
    def prepare(h):
        proj = pmm3(h, w_in)
        _, _, _, _, g, xr, gate = jnp.split(proj, EVEN_SPLITS, axis=-1)
        xc = centred_dwconv(xr, conv_w, conv_b).astype(jnp.float32)
        g_f = rglru_gates(xc, wa[0], ba[0], wx[0], bx[0], lam[0])
        g_b = rglru_gates(xc, wa[1], ba[1], wx[1], bx[1], lam[1])
        return proj, (g_f, g_b), (g, gate)

    def hgrn_bidir(proj, s_f, s_b):
        o_f, s_f = hgrn_scan(proj, lb[0:1], s_f, False)
        o_b, s_b = hgrn_scan(proj, lb[1:2], s_b, True)
        return o_f + o_b, s_f, s_b

    def merge(o_hg, h_rg, g, gate):
        bsz, t_len, _ = g.shape
        o_hg = o_hg.reshape(bsz, t_len, HG_HEADS, HG_DV)
        y_a = rms_norm(o_hg, hg_gain) * jax.nn.silu(g.astype(jnp.float32)).reshape(bsz, t_len, HG_HEADS, HG_DV)
        y_b = jax.nn.gelu(gate.astype(jnp.float32)) * h_rg
        y = jnp.concatenate([y_a.reshape(bsz, t_len, HG_WV), y_b], axis=-1)
        return pmm3(y.astype(g.dtype), w_out)

    hg_c, rg_c, gt_c = prepare(hc)
    hg_l, rg_l, gt_l = prepare(hl)
    bsz = hl.shape[0]
    s0 = jnp.zeros((bsz, HG_HEADS, HG_DK, HG_DV), jnp.float32)
    h0 = jnp.zeros((bsz, 1, RG_W), jnp.float32)
    o_c, s_f, s_b = hgrn_bidir(hg_c, s0, s0)
    r_c, h_f, h_b = rglru_bidir(*rg_c, h0, h0)
    o_l, _, _ = hgrn_bidir(hg_l, s_f, s_b)
    r_l, _, _ = rglru_bidir(*rg_l, h_f, h_b)
    return merge(o_l, r_l, *gt_l), merge(o_c, r_c, *gt_c)


def axial_rope(t_len):
    rows = t_len // GRID_W
    row = jnp.repeat(jnp.arange(rows, dtype=jnp.float32), GRID_W)
    col = jnp.tile(jnp.arange(GRID_W, dtype=jnp.float32), rows)
    n_freq = ATT_DH // 4
    inv = ROPE_BASE ** (-jnp.arange(n_freq, dtype=jnp.float32) / n_freq)
    ang = jnp.concatenate([row[:, None] * inv, col[:, None] * inv], axis=-1)
    return jnp.cos(ang), jnp.sin(ang)


def apply_rope(x, cos, sin):
    xf = x.astype(jnp.float32)
    x1, x2 = jnp.split(xf, 2, axis=-1)
    cs, sn = cos[None, :, None, :], sin[None, :, None, :]
    return jnp.concatenate([x1 * cs - x2 * sn, x2 * cs + x1 * sn], axis=-1).astype(x.dtype)


def sink_attention(q, k, v, sink, mask):
    s = jnp.einsum('bqhgd,bkhd->bhgqk', q, k).astype(jnp.float32) * (ATT_DH ** -0.5)
    if mask is not None:
        s = jnp.where(mask, s, -jnp.inf)
    sk = jnp.broadcast_to(sink.astype(jnp.float32)[None, :, :, None, None], s.shape[:-1] + (1,))
    p = jax.nn.softmax(jnp.concatenate([sk, s], axis=-1), axis=-1)[..., 1:]
    return jnp.einsum('bhgqk,bkhd->bqhgd', p.astype(v.dtype), v)


def windowed_attention(q, k, v, kc, vc, sink):
    bsz, t_len = q.shape[:2]
    n_blk = t_len // ATT_BLOCK
    n_ctx = kc.shape[1]
    pad = ((0, 0), (ATT_BLOCK, ATT_BLOCK), (0, 0), (0, 0))
    kp, vp = jnp.pad(k, pad), jnp.pad(v, pad)
    qb = q.reshape(bsz, n_blk, ATT_BLOCK, ATT_HKV, ATT_G, ATT_DH).transpose(1, 0, 2, 3, 4, 5)
    offs_q = jnp.arange(ATT_BLOCK)
    offs_k = jnp.arange(3 * ATT_BLOCK)
    ctx_mask = jnp.ones((ATT_BLOCK, n_ctx), dtype=bool)

    def block(args):
        n, qn = args
        kn = lax.dynamic_slice_in_dim(kp, n * ATT_BLOCK, 3 * ATT_BLOCK, axis=1)
        vn = lax.dynamic_slice_in_dim(vp, n * ATT_BLOCK, 3 * ATT_BLOCK, axis=1)
        qpos = n * ATT_BLOCK + offs_q
        kpos = (n - 1) * ATT_BLOCK + offs_k
        local = (jnp.abs(qpos[:, None] - kpos[None, :]) <= WINDOW) & (kpos >= 0)[None, :] & (kpos < t_len)[None, :]
        mask = jnp.concatenate([ctx_mask, local], axis=1)
        return sink_attention(qn, jnp.concatenate([kc, kn], axis=1), jnp.concatenate([vc, vn], axis=1), sink, mask)

    o = lax.map(block, (jnp.arange(n_blk), qb))
    return o.transpose(1, 0, 2, 3, 4, 5).reshape(bsz, t_len, ATT_WQ)


def fourier_mix(z):
    bsz, t_len, _ = z.shape
    zz = z.astype(jnp.float32).reshape(bsz, t_len, FN_GROUPS, FN_DIM)
    y = jnp.fft.fft2(zz, axes=(1, 3), norm='ortho').real
    return y.reshape(bsz, t_len, FN_W).astype(z.dtype)


def odd_mixer(hl, hc, w_in, w_out, q_gain, k_gain, sink, need_ctx):
    bsz, t_len, _ = hl.shape
    n_ctx = hc.shape[1]
    ql, kl, vl, fl = jnp.split(pmm3(hl, w_in), ODD_SPLITS, axis=-1)
    cos, sin = axial_rope(t_len)
    ql = apply_rope(rms_norm(ql.reshape(bsz, t_len, ATT_HQ, ATT_DH), q_gain), cos, sin)
    kl = apply_rope(rms_norm(kl.reshape(bsz, t_len, ATT_HKV, ATT_DH), k_gain), cos, sin)
    vl = vl.reshape(bsz, t_len, ATT_HKV, ATT_DH)
    kc, vc = jnp.split(pmm3(hc, w_in[:, ATT_WQ:ATT_WQ + 2 * ATT_WKV]), 2, axis=-1)
    kc = rms_norm(kc.reshape(bsz, n_ctx, ATT_HKV, ATT_DH), k_gain)
    vc = vc.reshape(bsz, n_ctx, ATT_HKV, ATT_DH)
    sink_g = sink.reshape(ATT_HKV, ATT_G)
    a_l = windowed_attention(ql.reshape(bsz, t_len, ATT_HKV, ATT_G, ATT_DH), kl, vl, kc, vc, sink_g)
    y_l = pmm3(jnp.concatenate([a_l, fourier_mix(fl)], axis=-1), w_out)
    if not need_ctx:
        return y_l, None
    qc = rms_norm(pmm3(hc, w_in[:, :ATT_WQ]).reshape(bsz, n_ctx, ATT_HKV, ATT_G, ATT_DH), q_gain)
    fc = pmm3(hc, w_in[:, ATT_WQ + 2 * ATT_WKV:])
    a_c = sink_attention(qc, kc, vc, sink_g, None).reshape(bsz, n_ctx, ATT_WQ)
    y_c = pmm3(jnp.concatenate([a_c, fourier_mix(fc)], axis=-1), w_out)
    return y_l, y_c


PEER_TA = 256
PEER_TG = 8
PEER_GPB = 2
PEER_SUB = 8
PEER_COLS = PK_HEADS * PK_TOPK * PEER_SUB
PEER_CHUNK = 4096
SC_W = 32
SC_IDX_BLOCK = 128


def _topk_rows(s, k):
    r, t = s.shape
    iota = lax.broadcasted_iota(jnp.int32, (r, t), 0)
    kiota = lax.broadcasted_iota(jnp.int32, (k, t), 0)
    vals = jnp.zeros((k, t), jnp.float32)
    rows = jnp.zeros((k, t), jnp.int32)
    for j in range(k):
        m = jnp.max(s, axis=0, keepdims=True)
        am = jnp.min(jnp.where(s == m, iota, r), axis=0, keepdims=True)
        vals = jnp.where(kiota == j, m, vals)
        rows = jnp.where(kiota == j, am, rows)
        s = jnp.where(iota == am, -jnp.inf, s)
    return vals, rows


def _route_kernel(x_ref, wq_ref, bq_ref, keys_ref, eidx_ref, gate_ref):
    xb = x_ref[...].astype(jnp.bfloat16)

    def head(h, carry):
        q = jnp.dot(xb, wq_ref[h], preferred_element_type=jnp.float32) + bq_ref[h]
        sv, si = [], []
        for p in range(2):
            qs = q[:, p * PK_DKH:(p + 1) * PK_DKH].astype(jnp.bfloat16)
            s_t = lax.dot_general(keys_ref[h, p], qs, NT_DIMS, preferred_element_type=jnp.float32)
            v, i = _topk_rows(s_t, PK_TOPK)
            sv.append(v)
            si.append(i)
        cand = [sv[0][0:1] + sv[1]]
        cidx = [si[0][0:1] * PK_NKEYS + si[1]]
        for a in range(1, 8):
            cand.append(sv[0][a:a + 1] + sv[1][0:8])
            cidx.append(si[0][a:a + 1] * PK_NKEYS + si[1][0:8])
        cand.append(sv[0][8:16] + sv[1][0:1])
        cidx.append(si[0][8:16] * PK_NKEYS + si[1][0:1])
        cand = jnp.concatenate(cand, axis=0)
        cidx = jnp.concatenate(cidx, axis=0)
        best, pos = _topk_rows(cand, PK_TOPK)
        r, t = cand.shape
        iota = lax.broadcasted_iota(jnp.int32, (r, t), 0)
        kiota = lax.broadcasted_iota(jnp.int32, (PK_TOPK, t), 0)
        eidx = jnp.zeros((PK_TOPK, t), jnp.int32)
        for j in range(PK_TOPK):
            e = jnp.max(jnp.where(iota == pos[j:j + 1], cidx, -1), axis=0, keepdims=True)
            eidx = jnp.where(kiota == j, e, eidx)
        ex = jnp.exp(best - best[0:1])
        eidx_ref[h] = eidx
        gate_ref[h] = ex / jnp.sum(ex, axis=0, keepdims=True)
        return carry

    lax.fori_loop(0, PK_HEADS, head, 0)


def peer_route(x, wq3, bq3, keys_b):
    n, d = x.shape
    return pl.pallas_call(
        _route_kernel,
        out_shape=(jax.ShapeDtypeStruct((PK_HEADS, PK_TOPK, n), jnp.int32),
                   jax.ShapeDtypeStruct((PK_HEADS, PK_TOPK, n), jnp.float32)),
        grid=(n // PEER_TA,),
        in_specs=[pl.BlockSpec((PEER_TA, d), lambda i: (i, 0)),
                  pl.BlockSpec((PK_HEADS, d, PK_DK), lambda i: (0, 0, 0)),
                  pl.BlockSpec((PK_HEADS, 1, PK_DK), lambda i: (0, 0, 0)),
                  pl.BlockSpec((PK_HEADS, 2, PK_NKEYS, PK_DKH), lambda i: (0, 0, 0, 0))],
        out_specs=(pl.BlockSpec((PK_HEADS, PK_TOPK, PEER_TA), lambda i: (0, 0, i)),
                   pl.BlockSpec((PK_HEADS, PK_TOPK, PEER_TA), lambda i: (0, 0, i))),
        compiler_params=pltpu.CompilerParams(dimension_semantics=("parallel",)),
        name="peer_route",
    )(x, wq3, bq3, keys_b)


def _lane_group_allsum(y, lane):
    k = 1
    while k < PEER_SUB:
        up = pltpu.roll(y, 128 - k, axis=1)
        dn = pltpu.roll(y, k, axis=1)
        y = y + jnp.where((lane & k) == 0, up, dn)
        k *= 2
    return y


def _mix_kernel(x2_ref, gate_ref, g_ref, o2_ref, hp_ref):
    lane = lax.broadcasted_iota(jnp.int32, (PEER_SUB, PEER_COLS), 1)
    sub = lax.broadcasted_iota(jnp.int32, (PEER_SUB, PEER_COLS), 0)
    same_s = (lane % PEER_SUB) == sub
    lane128 = lax.broadcasted_iota(jnp.int32, (PEER_TG, 128), 1)
    for g in range(PEER_GPB):
        for t in range(PEER_TG):
            tok = g * PEER_TG + t
            w = g_ref[pl.ds(tok * PEER_COLS, PEER_COLS), :]
            ub = lax.bitcast_convert_type(w << 16, jnp.float32).astype(jnp.bfloat16)
            x2 = x2_ref[pl.ds(tok * PEER_SUB, PEER_SUB), :].astype(jnp.bfloat16)
            p = lax.dot_general(x2, ub, NT_DIMS, preferred_element_type=jnp.float32)
            hp_ref[pl.ds(t, 1), :] = jnp.sum(jnp.where(same_s, p, 0.0), axis=0, keepdims=True)
        hp = hp_ref[...]
        hfull = jnp.concatenate(
            [_lane_group_allsum(hp[:, c * 128:(c + 1) * 128], lane128) for c in range(PEER_COLS // 128)], axis=1)
        act = jax.nn.gelu(hfull) * gate_ref[pl.ds(g * PEER_TG, PEER_TG), :]
        for t in range(PEER_TG):
            tok = g * PEER_TG + t
            a2 = jnp.where(same_s, act[t:t + 1, :], 0.0).astype(jnp.bfloat16)
            w = g_ref[pl.ds(tok * PEER_COLS, PEER_COLS), :]
            vb = lax.bitcast_convert_type(w & jnp.int32(-65536), jnp.float32).astype(jnp.bfloat16)
            o2_ref[pl.ds(tok * PEER_SUB, PEER_SUB), :] = jnp.dot(a2, vb, preferred_element_type=jnp.float32)


def peer_mix(x, gate, gathered):
    n, d = x.shape
    tb = PEER_TG * PEER_GPB
    x2 = x.reshape(n * PEER_SUB, 128)
    gate8 = jnp.repeat(gate, PEER_SUB, axis=1)
    g2 = gathered.reshape(n * PEER_COLS, 128)
    out2 = pl.pallas_call(
        _mix_kernel,
        out_shape=jax.ShapeDtypeStruct((n * PEER_SUB, 128), jnp.float32),
        grid=(n // tb,),
        in_specs=[pl.BlockSpec((tb * PEER_SUB, 128), lambda i: (i, 0)),
                  pl.BlockSpec((tb, PEER_COLS), lambda i: (i, 0)),
                  pl.BlockSpec((tb * PEER_COLS, 128), lambda i: (i, 0))],
        out_specs=pl.BlockSpec((tb * PEER_SUB, 128), lambda i: (i, 0)),
        scratch_shapes=[pltpu.VMEM((PEER_TG, PEER_COLS), jnp.float32)],
        compiler_params=pltpu.CompilerParams(dimension_semantics=("parallel",),
                                             vmem_limit_bytes=48 * 1024 * 1024),
        name="peer_mix",
    )(x2, gate8, g2)
    return out2.reshape(n, d)


def sc_gather(table, idxp):
    nblk = idxp.shape[1] // SC_IDX_BLOCK
    mesh = plsc.VectorSubcoreMesh(core_axis_name="c", subcore_axis_name="s")

    @functools.partial(pl.kernel, out_type=jax.ShapeDtypeStruct((nblk * SC_W, 8, 128), table.dtype), mesh=mesh)
    def gather_kernel(t_hbm, i_hbm, o_hbm):
        def body(i_vmem, o_vmem):
            pltpu.sync_copy(t_hbm.at[i_vmem.at[0, pl.ds(0, SC_W)]], o_vmem)

        pltpu.emit_pipeline(
            body, grid=(nblk,),
            in_specs=[pl.BlockSpec((1, SC_IDX_BLOCK), lambda i: (0, i))],
            out_specs=[pl.BlockSpec((SC_W, 8, 128), lambda i: (i, 0, 0))],
            core_axis_name=("c", "s"),
            dimension_semantics=(pltpu.PARALLEL,),
            trace_scopes=False,
        )(i_hbm, o_hbm)

    return gather_kernel(table, idxp)


def pack_table(u, v):
    ub = lax.bitcast_convert_type(u.astype(jnp.bfloat16), jnp.uint16).astype(jnp.uint32)
    vb = lax.bitcast_convert_type(v.astype(jnp.bfloat16), jnp.uint16).astype(jnp.uint32)
    w = lax.bitcast_convert_type(ub | (vb << 16), jnp.int32)
    return w.reshape(u.shape[0], 8, 128)


def peer(h, w_q, b_q, sub_keys, table):
    n, d = h.shape
    wq3 = w_q.reshape(d, PK_HEADS, PK_DK).transpose(1, 0, 2).astype(jnp.bfloat16)
    h_pad = jnp.concatenate([h, jnp.zeros((PEER_TA, d), h.dtype)], axis=0)
    eidx, gate = peer_route(h_pad, wq3, b_q.reshape(PK_HEADS, 1, PK_DK), sub_keys.astype(jnp.bfloat16))
    eidx = eidx.reshape(PK_HEADS * PK_TOPK, n + PEER_TA).T[:n]
    gate = gate.reshape(PK_HEADS * PK_TOPK, n + PEER_TA).T[:n]
    outs = []
    for c0 in range(0, n, PEER_CHUNK):
        c1 = min(n, c0 + PEER_CHUNK)
        e = eidx[c0:c1].reshape(-1, SC_W)
        idxp = jnp.pad(e, ((0, 0), (0, SC_IDX_BLOCK - SC_W))).reshape(1, -1)
        gathered = sc_gather(table, idxp)
        outs.append(peer_mix(h[c0:c1], gate[c0:c1], gathered))
    return jnp.concatenate(outs, axis=0)


def kernel(x, c, ctx, c_ctx, w_mod, b_mod, g_mix, g_ffn, e_w_in, e_w_out, hg_lb_logits, hg_gain, rg_conv_w,
           rg_conv_b, rg_wa, rg_ba, rg_wx, rg_bx, rg_lambda, o_w_in, o_w_out, q_gain, k_gain, sinks, p_wq,
           p_bq, p_keys, p_u, p_v):
    lb_all = jnp.cumsum(jax.nn.softmax(hg_lb_logits.astype(jnp.float32), axis=1), axis=1)
    tables = [pack_table(p_u[l], p_v[l]) for l in range(DEPTH)]
    lat, cx = x, ctx
    for l in range(DEPTH):
        last = l == DEPTH - 1
        j = l // 2
        m_lat = jnp.split((jax.nn.silu(c) @ w_mod[l] + b_mod[l])[:, None, :], N_MOD, axis=-1)
        m_ctx = jnp.split(jax.nn.silu(c_ctx) @ w_mod[l] + b_mod[l], N_MOD, axis=-1)
        hl = modulate(rms_norm(lat, g_mix[l]), m_lat[0], m_lat[1])
        hc = modulate(rms_norm(cx, g_mix[l]), m_ctx[0], m_ctx[1])
        if l % 2 == 0:
            yl, yc = even_mixer(hl, hc, e_w_in[j], e_w_out[j], lb_all[:, j], hg_gain[j], rg_conv_w[j], rg_conv_b[j],
                                rg_wa[j], rg_ba[j], rg_wx[j], rg_bx[j], rg_lambda[j])
        else:
            yl, yc = odd_mixer(hl, hc, o_w_in[j], o_w_out[j], q_gain[j], k_gain[j], sinks[j], not last)
        lat = lat + m_lat[2] * yl
        hl2 = modulate(rms_norm(lat, g_ffn[l]), m_lat[3], m_lat[4])
        n_lat = hl2.shape[0] * hl2.shape[1]
        if last:
            ff = peer(hl2.reshape(n_lat, hl2.shape[-1]), p_wq[l], p_bq[l], p_keys[l], tables[l])
            lat = lat + m_lat[5] * ff.reshape(lat.shape)
        else:
            cx = cx + m_ctx[2] * yc
            hc2 = modulate(rms_norm(cx, g_ffn[l]), m_ctx[3], m_ctx[4])
            tokens = jnp.concatenate([hl2.reshape(n_lat, hl2.shape[-1]), hc2.reshape(-1, hc2.shape[-1])], axis=0)
            ff = peer(tokens, p_wq[l], p_bq[l], p_keys[l], tables[l])
            lat = lat + m_lat[5] * ff[:n_lat].reshape(lat.shape)
            cx = cx + m_ctx[5] * ff[n_lat:].reshape(cx.shape)
    return lat
```

```python
import functools
import math

import jax
import jax.numpy as jnp
import numpy as np
from jax import lax
from jax.experimental import pallas as pl
from jax.experimental.pallas import tpu as pltpu
from jax.experimental.pallas import tpu_sc as plsc

D_MODEL = 1024
DEPTH = 2
GRID_W = 64
EPS = 1e-6
N_MOD = 6

HG_HEADS = 4
HG_DK = 128
HG_DV = 128
HG_WK = HG_HEADS * HG_DK
HG_WV = HG_HEADS * HG_DV
HG_CHUNK = 64
RG_W = 512
RG_BLOCKS = 8
RG_BW = RG_W // RG_BLOCKS
RG_C = 8.0
CONV_W = 4
EVEN_SPLITS = (HG_WK, HG_WK + HG_WV, 2 * HG_WK + HG_WV, 3 * HG_WK + HG_WV, 3 * HG_WK + 2 * HG_WV,
               3 * HG_WK + 2 * HG_WV + RG_W)

ATT_HQ = 12
ATT_HKV = 4
ATT_G = ATT_HQ // ATT_HKV
ATT_DH = 64
ATT_WQ = ATT_HQ * ATT_DH
ATT_WKV = ATT_HKV * ATT_DH
WINDOW = 128
ATT_BLOCK = 128
ROPE_BASE = 10000.0
FN_GROUPS = 4
FN_DIM = 64
FN_W = FN_GROUPS * FN_DIM
ODD_SPLITS = (ATT_WQ, ATT_WQ + ATT_WKV, ATT_WQ + 2 * ATT_WKV)

PK_HEADS = 8
PK_NKEYS = 128
PK_DK = 256
PK_DKH = PK_DK // 2
PK_TOPK = 16
PK_CHUNK = 128


def _mm_kernel(a_ref, b_ref, o_ref):
    o_ref[...] = jnp.dot(a_ref[...].astype(jnp.bfloat16), b_ref[...].astype(jnp.bfloat16),
                         preferred_element_type=jnp.float32)


def pmm(a, b, tm=512, tn=512):
    m, k = a.shape
    n = b.shape[1]
    tm = next(t for t in (tm, 256, 128, m) if m % t == 0)
    tn = next(t for t in (tn, 256, 128, n) if n % t == 0)
    return pl.pallas_call(
        _mm_kernel,
        out_shape=jax.ShapeDtypeStruct((m, n), jnp.float32),
        grid=(m // tm, n // tn),
        in_specs=[pl.BlockSpec((tm, k), lambda i, j: (i, 0)),
                  pl.BlockSpec((k, tn), lambda i, j: (0, j))],
        out_specs=pl.BlockSpec((tm, tn), lambda i, j: (i, j)),
        compiler_params=pltpu.CompilerParams(dimension_semantics=("arbitrary", "arbitrary")),
        name="pmm",
    )(a, b)


def pmm3(a, b):
    bsz, t, k = a.shape
    return pmm(a.reshape(bsz * t, k), b).reshape(bsz, t, b.shape[1])


def rms_norm(x, g):
    xf = x.astype(jnp.float32)
    y = xf * lax.rsqrt(jnp.mean(xf * xf, axis=-1, keepdims=True) + EPS)
    return (y * g.astype(jnp.float32)).astype(x.dtype)


def modulate(h, shift, scale):
    return h * (1.0 + scale) + shift


def centred_dwconv(x, w, b):
    left = CONV_W // 2
    y = lax.conv_general_dilated(x, w.astype(x.dtype)[:, None, :], (1,), [(left, CONV_W - 1 - left)],
                                 dimension_numbers=('NWC', 'WIO', 'NWC'), feature_group_count=x.shape[-1])
    return y + b.astype(x.dtype)


HG_SUB = 16
NT_DIMS = (((1,), (1,)), ((), ()))
TN_DIMS = (((0,), (0,)), ((), ()))
HG_QCOL, HG_ICOL, HG_FCOL = 0, HG_HEADS, 2 * HG_HEADS


def _split3(x):
    hi = x.astype(jnp.bfloat16)
    r = x - hi.astype(jnp.float32)
    mid = r.astype(jnp.bfloat16)
    lo = (r - mid.astype(jnp.float32)).astype(jnp.bfloat16)
    return hi, mid, lo


def _dot01(m01, x):
    return sum(jnp.dot(m01, p, preferred_element_type=jnp.float32) for p in _split3(x))


def _hgrn_kernel(qr_ref, ir_ref, fr_ref, lb_ref, s0_ref, o_ref, send_ref, st_scr, *, reverse, n_chunks):
    c = pl.program_id(2)

    @pl.when(c == 0)
    def _():
        st_scr[...] = s0_ref[0, 0]

    n = HG_CHUNK
    row = lax.broadcasted_iota(jnp.int32, (n, n), 0)
    col = lax.broadcasted_iota(jnp.int32, (n, n), 1)
    lb = lb_ref[...]
    f = lb + (1.0 - lb) * jax.nn.sigmoid(fr_ref[0])
    k = 1.0 - f
    g = jnp.log(f)
    q = jax.nn.silu(qr_ref[0])
    v = ir_ref[0]
    if reverse:
        flipm = (row + col == n - 1).astype(jnp.bfloat16)
        q, k, v, g = (_dot01(flipm, a) for a in (q, k, v, g))
    b = _dot01((col <= row).astype(jnp.bfloat16), g)
    st = st_scr[...]
    o = lax.dot_general((q * jnp.exp(b)).astype(jnp.bfloat16), st.astype(jnp.bfloat16), NT_DIMS,
                        preferred_element_type=jnp.float32)
    srow = lax.broadcasted_iota(jnp.int32, (HG_SUB, HG_DK), 0)
    blocks = []
    for r0 in range(0, n, HG_SUB):
        b_i, q_i, k_i, v_i = (a[r0:r0 + HG_SUB] for a in (b, q, k, v))
        o_i = o[r0:r0 + HG_SUB]
        if r0 > 0:
            ref = b[r0 - 1:r0]
            a_q = (q_i * jnp.exp(b_i - ref)).astype(jnp.bfloat16)
            a_k = (k[0:r0] * jnp.exp(ref - b[0:r0])).astype(jnp.bfloat16)
            att = lax.dot_general(a_q, a_k, NT_DIMS, preferred_element_type=jnp.float32)
            o_i = o_i + jnp.dot(att.astype(jnp.bfloat16), v[0:r0].astype(jnp.bfloat16),
                                preferred_element_type=jnp.float32)
        diag = jnp.zeros((HG_SUB, HG_DV), jnp.float32)
        for t in range(HG_SUB):
            e = jnp.exp(jnp.minimum(b_i[t:t + 1] - b_i, 0.0))
            w = jnp.sum(jnp.where(srow <= t, q_i[t:t + 1] * k_i * e, 0.0), axis=1, keepdims=True)
            diag = jnp.where(srow == t, jnp.sum(w * v_i, axis=0, keepdims=True), diag)
        blocks.append(o_i + diag)
    o_full = jnp.concatenate(blocks, axis=0)
    if reverse:
        o_full = _dot01(flipm, o_full)
    o_ref[0] = o_full
    b_end = b[n - 1:n]
    k_dec = (k * jnp.exp(b_end - b)).astype(jnp.bfloat16)
    st_new = st * jnp.exp(b_end) + lax.dot_general(v.astype(jnp.bfloat16), k_dec, TN_DIMS,
                                                   preferred_element_type=jnp.float32)
    st_scr[...] = st_new

    @pl.when(c == n_chunks - 1)
    def _():
        send_ref[0, 0] = st_new


def hgrn_scan(proj, lb, s0t, reverse):
    bsz, t_len, _ = proj.shape
    n_chunks = t_len // HG_CHUNK
    fcol = HG_FCOL + (HG_HEADS if reverse else 0)

    def tmap(c):
        return n_chunks - 1 - c if reverse else c

    blk = (1, HG_CHUNK, HG_DK)
    return pl.pallas_call(
        functools.partial(_hgrn_kernel, reverse=reverse, n_chunks=n_chunks),
        out_shape=(jax.ShapeDtypeStruct((bsz, t_len, HG_HEADS * HG_DV), jnp.float32),
                   jax.ShapeDtypeStruct((bsz, HG_HEADS, HG_DV, HG_DK), jnp.float32)),
        grid=(bsz, HG_HEADS, n_chunks),
        in_specs=[pl.BlockSpec(blk, lambda b, h, c: (b, tmap(c), HG_QCOL + h)),
                  pl.BlockSpec(blk, lambda b, h, c: (b, tmap(c), HG_ICOL + h)),
                  pl.BlockSpec(blk, lambda b, h, c: (b, tmap(c), fcol + h)),
                  pl.BlockSpec((1, HG_DK), lambda b, h, c: (0, h)),
                  pl.BlockSpec((1, 1, HG_DV, HG_DK), lambda b, h, c: (b, h, 0, 0))],
        out_specs=(pl.BlockSpec(blk, lambda b, h, c: (b, tmap(c), h)),
                   pl.BlockSpec((1, 1, HG_DV, HG_DK), lambda b, h, c: (b, h, 0, 0))),
        scratch_shapes=[pltpu.VMEM((HG_DV, HG_DK), jnp.float32)],
        compiler_params=pltpu.CompilerParams(dimension_semantics=("arbitrary", "arbitrary", "arbitrary")),
        name="hgrn_scan",
    )(proj, proj, proj, lb, s0t)


def block_diag(x, w):
    bsz, t_len, _ = x.shape
    y = jnp.einsum('btnc,ncd->btnd', x.reshape(bsz, t_len, RG_BLOCKS, RG_BW), w)
    return y.reshape(bsz, t_len, RG_W)


def rglru_gates(xc, wa, ba, wx, bx, lam):
    r = jax.nn.sigmoid(block_diag(xc, wa) + ba)
    i = jax.nn.sigmoid(block_diag(xc, wx) + bx)
    log_a = -RG_C * r * jax.nn.softplus(-lam.astype(jnp.float32))
    u = jnp.sqrt(-jnp.expm1(2.0 * log_a)) * (i * xc)
    return log_a, u


RG_TB = 256
RG_SUB = 16


def _rglru_kernel(la_ref, u_ref, h0_ref, h_ref, hend_ref, h_scr, *, reverse, n_blocks):
    c = pl.program_id(1)

    @pl.when(c == 0)
    def _():
        h_scr[...] = h0_ref[0]

    row = lax.broadcasted_iota(jnp.int32, (RG_SUB, RG_SUB), 0)
    col = lax.broadcasted_iota(jnp.int32, (RG_SUB, RG_SUB), 1)
    tri = ((col >= row) if reverse else (col <= row)).astype(jnp.bfloat16)
    srow = lax.broadcasted_iota(jnp.int32, (RG_SUB, RG_W), 0)
    n_sub = h_ref.shape[1] // RG_SUB

    def sub_block(j, h_prev):
        jj = n_sub - 1 - j if reverse else j
        r0 = pl.multiple_of(jj * RG_SUB, RG_SUB)
        la = la_ref[0, pl.ds(r0, RG_SUB), :]
        uu = u_ref[0, pl.ds(r0, RG_SUB), :]
        acc_la = _dot01(tri, la)
        out = jnp.zeros((RG_SUB, RG_W), jnp.float32)
        for t in range(RG_SUB):
            seen = (srow >= t) if reverse else (srow <= t)
            e = jnp.exp(jnp.minimum(acc_la[t:t + 1] - acc_la, 0.0))
            h_t = jnp.sum(jnp.where(seen, e * uu, 0.0), axis=0, keepdims=True) + jnp.exp(acc_la[t:t + 1]) * h_prev
            out = jnp.where(srow == t, h_t, out)
        h_ref[0, pl.ds(r0, RG_SUB), :] = out
        last = 0 if reverse else RG_SUB - 1
        return out[last:last + 1]

    h_last = lax.fori_loop(0, n_sub, sub_block, h_scr[...])
    h_scr[...] = h_last

    @pl.when(c == n_blocks - 1)
    def _():
        hend_ref[0] = h_last


def rglru_scan(log_a, u, h0, reverse):
    bsz, t_len, w = log_a.shape
    tb = min(RG_TB, t_len)
    n_blocks = t_len // tb

    def tmap(c):
        return n_blocks - 1 - c if reverse else c

    blk = (1, tb, w)
    return pl.pallas_call(
        functools.partial(_rglru_kernel, reverse=reverse, n_blocks=n_blocks),
        out_shape=(jax.ShapeDtypeStruct((bsz, t_len, w), jnp.float32),
                   jax.ShapeDtypeStruct((bsz, 1, w), jnp.float32)),
        grid=(bsz, n_blocks),
        in_specs=[pl.BlockSpec(blk, lambda b, c: (b, tmap(c), 0)),
                  pl.BlockSpec(blk, lambda b, c: (b, tmap(c), 0)),
                  pl.BlockSpec((1, 1, w), lambda b, c: (b, 0, 0))],
        out_specs=(pl.BlockSpec(blk, lambda b, c: (b, tmap(c), 0)),
                   pl.BlockSpec((1, 1, w), lambda b, c: (b, 0, 0))),
        scratch_shapes=[pltpu.VMEM((1, w), jnp.float32)],
        compiler_params=pltpu.CompilerParams(dimension_semantics=("arbitrary", "arbitrary")),
        name="rglru_scan",
    )(log_a, u, h0)


def rglru_bidir(g_f, g_b, h_f, h_b):
    y_f, h_f = rglru_scan(g_f[0], g_f[1], h_f, False)
    y_b, h_b = rglru_scan(g_b[0], g_b[1], h_b, True)
    return y_f + y_b, h_f, h_b


def even_mixer(hl, hc, w_in, w_out, lb, hg_gain, conv_w, conv_b, wa, ba, wx, bx, lam):
    def prepare(h):
        proj = pmm3(h, w_in)
        _, _, _, _, g, xr, gate = jnp.split(proj, EVEN_SPLITS, axis=-1)
        xc = centred_dwconv(xr, conv_w, conv_b).astype(jnp.float32)
        g_f = rglru_gates(xc, wa[0], ba[0], wx[0], bx[0], lam[0])
        g_b = rglru_gates(xc, wa[1], ba[1], wx[1], bx[1], lam[1])
        return proj, (g_f, g_b), (g, gate)

    def hgrn_bidir(proj, s_f, s_b):
        o_f, s_f = hgrn_scan(proj, lb[0:1], s_f, False)
        o_b, s_b = hgrn_scan(proj, lb[1:2], s_b, True)
        return o_f + o_b, s_f, s_b

    def merge(o_hg, h_rg, g, gate):
        bsz, t_len, _ = g.shape
        o_hg = o_hg.reshape(bsz, t_len, HG_HEADS, HG_DV)
        y_a = rms_norm(o_hg, hg_gain) * jax.nn.silu(g.astype(jnp.float32)).reshape(bsz, t_len, HG_HEADS, HG_DV)
        y_b = jax.nn.gelu(gate.astype(jnp.float32)) * h_rg
        y = jnp.concatenate([y_a.reshape(bsz, t_len, HG_WV), y_b], axis=-1)
        return pmm3(y.astype(g.dtype), w_out)

    hg_c, rg_c, gt_c = prepare(hc)
    hg_l, rg_l, gt_l = prepare(hl)
    bsz = hl.shape[0]
    s0 = jnp.zeros((bsz, HG_HEADS, HG_DK, HG_DV), jnp.float32)
    h0 = jnp.zeros((bsz, 1, RG_W), jnp.float32)
    o_c, s_f, s_b = hgrn_bidir(hg_c, s0, s0)
    r_c, h_f, h_b = rglru_bidir(*rg_c, h0, h0)
    o_l, _, _ = hgrn_bidir(hg_l, s_f, s_b)
    r_l, _, _ = rglru_bidir(*rg_l, h_f, h_b)
    return merge(o_l, r_l, *gt_l), merge(o_c, r_c, *gt_c)


def axial_rope(t_len):
    rows = t_len // GRID_W
    row = jnp.repeat(jnp.arange(rows, dtype=jnp.float32), GRID_W)
    col = jnp.tile(jnp.arange(GRID_W, dtype=jnp.float32), rows)
    n_freq = ATT_DH // 4
    inv = ROPE_BASE ** (-jnp.arange(n_freq, dtype=jnp.float32) / n_freq)
    ang = jnp.concatenate([row[:, None] * inv, col[:, None] * inv], axis=-1)
    return jnp.cos(ang), jnp.sin(ang)


def apply_rope(x, cos, sin):
    xf = x.astype(jnp.float32)
    x1, x2 = jnp.split(xf, 2, axis=-1)
    cs, sn = cos[None, :, None, :], sin[None, :, None, :]
    return jnp.concatenate([x1 * cs - x2 * sn, x2 * cs + x1 * sn], axis=-1).astype(x.dtype)


def sink_attention(q, k, v, sink, mask):
    s = jnp.einsum('bqhgd,bkhd->bhgqk', q, k).astype(jnp.float32) * (ATT_DH ** -0.5)
    if mask is not None:
        s = jnp.where(mask, s, -jnp.inf)
    sk = jnp.broadcast_to(sink.astype(jnp.float32)[None, :, :, None, None], s.shape[:-1] + (1,))
    p = jax.nn.softmax(jnp.concatenate([sk, s], axis=-1), axis=-1)[..., 1:]
    return jnp.einsum('bhgqk,bkhd->bqhgd', p.astype(v.dtype), v)


def windowed_attention(q, k, v, kc, vc, sink):
    bsz, t_len = q.shape[:2]
    n_blk = t_len // ATT_BLOCK
    n_ctx = kc.shape[1]
    pad = ((0, 0), (ATT_BLOCK, ATT_BLOCK), (0, 0), (0, 0))
    kp, vp = jnp.pad(k, pad), jnp.pad(v, pad)
    qb = q.reshape(bsz, n_blk, ATT_BLOCK, ATT_HKV, ATT_G, ATT_DH).transpose(1, 0, 2, 3, 4, 5)
    offs_q = jnp.arange(ATT_BLOCK)
    offs_k = jnp.arange(3 * ATT_BLOCK)
    ctx_mask = jnp.ones((ATT_BLOCK, n_ctx), dtype=bool)

    def block(args):
        n, qn = args
        kn = lax.dynamic_slice_in_dim(kp, n * ATT_BLOCK, 3 * ATT_BLOCK, axis=1)
        vn = lax.dynamic_slice_in_dim(vp, n * ATT_BLOCK, 3 * ATT_BLOCK, axis=1)
        qpos = n * ATT_BLOCK + offs_q
        kpos = (n - 1) * ATT_BLOCK + offs_k
        local = (jnp.abs(qpos[:, None] - kpos[None, :]) <= WINDOW) & (kpos >= 0)[None, :] & (kpos < t_len)[None, :]
        mask = jnp.concatenate([ctx_mask, local], axis=1)
        return sink_attention(qn, jnp.concatenate([kc, kn], axis=1), jnp.concatenate([vc, vn], axis=1), sink, mask)

    o = lax.map(block, (jnp.arange(n_blk), qb))
    return o.transpose(1, 0, 2, 3, 4, 5).reshape(bsz, t_len, ATT_WQ)


def fourier_mix(z):
    bsz, t_len, _ = z.shape
    zz = z.astype(jnp.float32).reshape(bsz, t_len, FN_GROUPS, FN_DIM)
    y = jnp.fft.fft2(zz, axes=(1, 3), norm='ortho').real
    return y.reshape(bsz, t_len, FN_W).astype(z.dtype)


def odd_mixer(hl, hc, w_in, w_out, q_gain, k_gain, sink, need_ctx):
    bsz, t_len, _ = hl.shape
    n_ctx = hc.shape[1]
    ql, kl, vl, fl = jnp.split(pmm3(hl, w_in), ODD_SPLITS, axis=-1)
    cos, sin = axial_rope(t_len)
    ql = apply_rope(rms_norm(ql.reshape(bsz, t_len, ATT_HQ, ATT_DH), q_gain), cos, sin)
    kl = apply_rope(rms_norm(kl.reshape(bsz, t_len, ATT_HKV, ATT_DH), k_gain), cos, sin)
    vl = vl.reshape(bsz, t_len, ATT_HKV, ATT_DH)
    kc, vc = jnp.split(pmm3(hc, w_in[:, ATT_WQ:ATT_WQ + 2 * ATT_WKV]), 2, axis=-1)
    kc = rms_norm(kc.reshape(bsz, n_ctx, ATT_HKV, ATT_DH), k_gain)
    vc = vc.reshape(bsz, n_ctx, ATT_HKV, ATT_DH)
    sink_g = sink.reshape(ATT_HKV, ATT_G)
    a_l = windowed_attention(ql.reshape(bsz, t_len, ATT_HKV, ATT_G, ATT_DH), kl, vl, kc, vc, sink_g)
    y_l = pmm3(jnp.concatenate([a_l, fourier_mix(fl)], axis=-1), w_out)
    if not need_ctx:
        return y_l, None
    qc = rms_norm(pmm3(hc, w_in[:, :ATT_WQ]).reshape(bsz, n_ctx, ATT_HKV, ATT_G, ATT_DH), q_gain)
    fc = pmm3(hc, w_in[:, ATT_WQ + 2 * ATT_WKV:])
    a_c = sink_attention(qc, kc, vc, sink_g, None).reshape(bsz, n_ctx, ATT_WQ)
    y_c = pmm3(jnp.concatenate([a_c, fourier_mix(fc)], axis=-1), w_out)
    return y_l, y_c


PEER_TA = 256
PEER_TG = 8
PEER_GPB = 2
PEER_SUB = 8
PEER_COLS = PK_HEADS * PK_TOPK * PEER_SUB
PEER_CHUNK = 4096
SC_W = 32
SC_IDX_BLOCK = 128


def _topk_rows(s, k):
    r, t = s.shape
    iota = lax.broadcasted_iota(jnp.int32, (r, t), 0)
    kiota = lax.broadcasted_iota(jnp.int32, (k, t), 0)
    vals = jnp.zeros((k, t), jnp.float32)
    rows = jnp.zeros((k, t), jnp.int32)
    for j in range(k):
        m = jnp.max(s, axis=0, keepdims=True)
        am = jnp.min(jnp.where(s == m, iota, r), axis=0, keepdims=True)
        vals = jnp.where(kiota == j, m, vals)
        rows = jnp.where(kiota == j, am, rows)
        s = jnp.where(iota == am, -jnp.inf, s)
    return vals, rows


def _route_kernel(x_ref, wq_ref, bq_ref, keys_ref, eidx_ref, gate_ref):
    xb = x_ref[...].astype(jnp.bfloat16)

    def head(h, carry):
        q = jnp.dot(xb, wq_ref[h], preferred_element_type=jnp.float32) + bq_ref[h]
        sv, si = [], []
        for p in range(2):
            qs = q[:, p * PK_DKH:(p + 1) * PK_DKH].astype(jnp.bfloat16)
            s_t = lax.dot_general(keys_ref[h, p], qs, NT_DIMS, preferred_element_type=jnp.float32)
            v, i = _topk_rows(s_t, PK_TOPK)
            sv.append(v)
            si.append(i)
        cand = [sv[0][0:1] + sv[1]]
        cidx = [si[0][0:1] * PK_NKEYS + si[1]]
        for a in range(1, 8):
            cand.append(sv[0][a:a + 1] + sv[1][0:8])
            cidx.append(si[0][a:a + 1] * PK_NKEYS + si[1][0:8])
        cand.append(sv[0][8:16] + sv[1][0:1])
        cidx.append(si[0][8:16] * PK_NKEYS + si[1][0:1])
        cand = jnp.concatenate(cand, axis=0)
        cidx = jnp.concatenate(cidx, axis=0)
        best, pos = _topk_rows(cand, PK_TOPK)
        r, t = cand.shape
        iota = lax.broadcasted_iota(jnp.int32, (r, t), 0)
        kiota = lax.broadcasted_iota(jnp.int32, (PK_TOPK, t), 0)
        eidx = jnp.zeros((PK_TOPK, t), jnp.int32)
        for j in range(PK_TOPK):
            e = jnp.max(jnp.where(iota == pos[j:j + 1], cidx, -1), axis=0, keepdims=True)
            eidx = jnp.where(kiota == j, e, eidx)
        ex = jnp.exp(best - best[0:1])
        eidx_ref[h] = eidx
        gate_ref[h] = ex / jnp.sum(ex, axis=0, keepdims=True)
        return carry

    lax.fori_loop(0, PK_HEADS, head, 0)


def peer_route(x, wq3, bq3, keys_b):
    n, d = x.shape
    return pl.pallas_call(
        _route_kernel,
        out_shape=(jax.ShapeDtypeStruct((PK_HEADS, PK_TOPK, n), jnp.int32),
                   jax.ShapeDtypeStruct((PK_HEADS, PK_TOPK, n), jnp.float32)),
        grid=(n // PEER_TA,),
        in_specs=[pl.BlockSpec((PEER_TA, d), lambda i: (i, 0)),
                  pl.BlockSpec((PK_HEADS, d, PK_DK), lambda i: (0, 0, 0)),
                  pl.BlockSpec((PK_HEADS, 1, PK_DK), lambda i: (0, 0, 0)),
                  pl.BlockSpec((PK_HEADS, 2, PK_NKEYS, PK_DKH), lambda i: (0, 0, 0, 0))],
        out_specs=(pl.BlockSpec((PK_HEADS, PK_TOPK, PEER_TA), lambda i: (0, 0, i)),
                   pl.BlockSpec((PK_HEADS, PK_TOPK, PEER_TA), lambda i: (0, 0, i))),
        compiler_params=pltpu.CompilerParams(dimension_semantics=("arbitrary",)),
        name="peer_route",
    )(x, wq3, bq3, keys_b)


def _lane_group_allsum(y, lane):
    k = 1
    while k < PEER_SUB:
        up = pltpu.roll(y, 128 - k, axis=1)
        dn = pltpu.roll(y, k, axis=1)
        y = y + jnp.where((lane & k) == 0, up, dn)
        k *= 2
    return y


def _mix_kernel(x2_ref, gate_ref, g_ref, o2_ref, hp_ref):
    lane = lax.broadcasted_iota(jnp.int32, (PEER_SUB, PEER_COLS), 1)
    sub = lax.broadcasted_iota(jnp.int32, (PEER_SUB, PEER_COLS), 0)
    same_s = (lane % PEER_SUB) == sub
    lane128 = lax.broadcasted_iota(jnp.int32, (PEER_TG, 128), 1)
    for g in range(PEER_GPB):
        for t in range(PEER_TG):
            tok = g * PEER_TG + t
            w = g_ref[pl.ds(tok * PEER_COLS, PEER_COLS), :]
            ub = lax.bitcast_convert_type(w << 16, jnp.float32).astype(jnp.bfloat16)
            x2 = x2_ref[pl.ds(tok * PEER_SUB, PEER_SUB), :].astype(jnp.bfloat16)
            p = lax.dot_general(x2, ub, NT_DIMS, preferred_element_type=jnp.float32)
            hp_ref[pl.ds(t, 1), :] = jnp.sum(jnp.where(same_s, p, 0.0), axis=0, keepdims=True)
        hp = hp_ref[...]
        hfull = jnp.concatenate(
            [_lane_group_allsum(hp[:, c * 128:(c + 1) * 128], lane128) for c in range(PEER_COLS // 128)], axis=1)
        act = jax.nn.gelu(hfull) * gate_ref[pl.ds(g * PEER_TG, PEER_TG), :]
        for t in range(PEER_TG):
            tok = g * PEER_TG + t
            a2 = jnp.where(same_s, act[t:t + 1, :], 0.0).astype(jnp.bfloat16)
            w = g_ref[pl.ds(tok * PEER_COLS, PEER_COLS), :]
            vb = lax.bitcast_convert_type(w & jnp.int32(-65536), jnp.float32).astype(jnp.bfloat16)
            o2_ref[pl.ds(tok * PEER_SUB, PEER_SUB), :] = jnp.dot(a2, vb, preferred_element_type=jnp.float32)


def peer_mix(x, gate, gathered):
    n, d = x.shape
    tb = PEER_TG * PEER_GPB
    x2 = x.reshape(n * PEER_SUB, 128)
    gate8 = jnp.repeat(gate, PEER_SUB, axis=1)
    g2 = gathered.reshape(n * PEER_COLS, 128)
    out2 = pl.pallas_call(
        _mix_kernel,
        out_shape=jax.ShapeDtypeStruct((n * PEER_SUB, 128), jnp.float32),
        grid=(n // tb,),
        in_specs=[pl.BlockSpec((tb * PEER_SUB, 128), lambda i: (i, 0)),
                  pl.BlockSpec((tb, PEER_COLS), lambda i: (i, 0)),
                  pl.BlockSpec((tb * PEER_COLS, 128), lambda i: (i, 0))],
        out_specs=pl.BlockSpec((tb * PEER_SUB, 128), lambda i: (i, 0)),
        scratch_shapes=[pltpu.VMEM((PEER_TG, PEER_COLS), jnp.float32)],
        compiler_params=pltpu.CompilerParams(dimension_semantics=("arbitrary",),
                                             vmem_limit_bytes=48 * 1024 * 1024),
        name="peer_mix",
    )(x2, gate8, g2)
    return out2.reshape(n, d)


def sc_gather(table, idxp):
    nblk = idxp.shape[1] // SC_IDX_BLOCK
    mesh = plsc.VectorSubcoreMesh(core_axis_name="c", subcore_axis_name="s")

    @functools.partial(pl.kernel, out_type=jax.ShapeDtypeStruct((nblk * SC_W, 8, 128), table.dtype), mesh=mesh)
    def gather_kernel(t_hbm, i_hbm, o_hbm):
        def body(i_vmem, o_vmem):
            pltpu.sync_copy(t_hbm.at[i_vmem.at[0, pl.ds(0, SC_W)]], o_vmem)

        pltpu.emit_pipeline(
            body, grid=(nblk,),
            in_specs=[pl.BlockSpec((1, SC_IDX_BLOCK), lambda i: (0, i))],
            out_specs=[pl.BlockSpec((SC_W, 8, 128), lambda i: (i, 0, 0))],
            core_axis_name=("c", "s"),
            dimension_semantics=(pltpu.PARALLEL,),
            trace_scopes=False,
        )(i_hbm, o_hbm)

    return gather_kernel(table, idxp)


def pack_table(u, v):
    ub = lax.bitcast_convert_type(u.astype(jnp.bfloat16), jnp.uint16).astype(jnp.uint32)
    vb = lax.bitcast_convert_type(v.astype(jnp.bfloat16), jnp.uint16).astype(jnp.uint32)
    w = lax.bitcast_convert_type(ub | (vb << 16), jnp.int32)
    return w.reshape(u.shape[0], 8, 128)


def peer(h, w_q, b_q, sub_keys, table):
    n, d = h.shape
    wq3 = w_q.reshape(d, PK_HEADS, PK_DK).transpose(1, 0, 2).astype(jnp.bfloat16)
    eidx, gate = peer_route(h, wq3, b_q.reshape(PK_HEADS, 1, PK_DK), sub_keys.astype(jnp.bfloat16))
    eidx = eidx.reshape(PK_HEADS * PK_TOPK, n).T
    gate = gate.reshape(PK_HEADS * PK_TOPK, n).T
    outs = []
    for c0 in range(0, n, PEER_CHUNK):
        c1 = min(n, c0 + PEER_CHUNK)
        e = eidx[c0:c1].reshape(-1, SC_W)
        idxp = jnp.pad(e, ((0, 0), (0, SC_IDX_BLOCK - SC_W))).reshape(1, -1)
        gathered = sc_gather(table, idxp)
        outs.append(peer_mix(h[c0:c1], gate[c0:c1], gathered))
    return jnp.concatenate(outs, axis=0)


def kernel(x, c, ctx, c_ctx, w_mod, b_mod, g_mix, g_ffn, e_w_in, e_w_out, hg_lb_logits, hg_gain, rg_conv_w,
           rg_conv_b, rg_wa, rg_ba, rg_wx, rg_bx, rg_lambda, o_w_in, o_w_out, q_gain, k_gain, sinks, p_wq,
           p_bq, p_keys, p_u, p_v):
    lb_all = jnp.cumsum(jax.nn.softmax(hg_lb_logits.astype(jnp.float32), axis=1), axis=1)
    tables = [pack_table(p_u[l], p_v[l]) for l in range(DEPTH)]
    lat, cx = x, ctx
    for l in range(DEPTH):
        last = l == DEPTH - 1
        j = l // 2
        m_lat = jnp.split((jax.nn.silu(c) @ w_mod[l] + b_mod[l])[:, None, :], N_MOD, axis=-1)
        m_ctx = jnp.split(jax.nn.silu(c_ctx) @ w_mod[l] + b_mod[l], N_MOD, axis=-1)
        hl = modulate(rms_norm(lat, g_mix[l]), m_lat[0], m_lat[1])
        hc = modulate(rms_norm(cx, g_mix[l]), m_ctx[0], m_ctx[1])
        if l % 2 == 0:
            yl, yc = even_mixer(hl, hc, e_w_in[j], e_w_out[j], lb_all[:, j], hg_gain[j], rg_conv_w[j], rg_conv_b[j],
                                rg_wa[j], rg_ba[j], rg_wx[j], rg_bx[j], rg_lambda[j])
        else:
            yl, yc = odd_mixer(hl, hc, o_w_in[j], o_w_out[j], q_gain[j], k_gain[j], sinks[j], not last)
        lat = lat + m_lat[2] * yl
        hl2 = modulate(rms_norm(lat, g_ffn[l]), m_lat[3], m_lat[4])
        n_lat = hl2.shape[0] * hl2.shape[1]
        if last:
            ff = peer(hl2.reshape(n_lat, hl2.shape[-1]), p_wq[l], p_bq[l], p_keys[l], tables[l])
            lat = lat + m_lat[5] * ff.reshape(lat.shape)
        else:
            cx = cx + m_ctx[2] * yc
            hc2 = modulate(rms_norm(cx, g_ffn[l]), m_ctx[3], m_ctx[4])
            tokens = jnp.concatenate([hl2.reshape(n_lat, hl2.shape[-1]), hc2.reshape(-1, hc2.shape[-1])], axis=0)
            ff = peer(tokens, p_wq[l], p_bq[l], p_keys[l], tables[l])
            lat = lat + m_lat[5] * ff[:n_lat].reshape(lat.shape)
            cx = cx + m_ctx[5] * ff[n_lat:].reshape(cx.shape)
    return lat
```

```python
import functools
import math

import jax
import jax.numpy as jnp
import numpy as np
from jax import lax
from jax.experimental import pallas as pl
from jax.experimental.pallas import tpu as pltpu
from jax.experimental.pallas import tpu_sc as plsc

D_MODEL = 1024
DEPTH = 2
GRID_W = 64
EPS = 1e-6
N_MOD = 6

HG_HEADS = 4
HG_DK = 128
HG_DV = 128
HG_WK = HG_HEADS * HG_DK
HG_WV = HG_HEADS * HG_DV
HG_CHUNK = 64
RG_W = 512
RG_BLOCKS = 8
RG_BW = RG_W // RG_BLOCKS
RG_C = 8.0
CONV_W = 4
EVEN_SPLITS = (HG_WK, HG_WK + HG_WV, 2 * HG_WK + HG_WV, 3 * HG_WK + HG_WV, 3 * HG_WK + 2 * HG_WV,
               3 * HG_WK + 2 * HG_WV + RG_W)

ATT_HQ = 12
ATT_HKV = 4
ATT_G = ATT_HQ // ATT_HKV
ATT_DH = 64
ATT_WQ = ATT_HQ * ATT_DH
ATT_WKV = ATT_HKV * ATT_DH
WINDOW = 128
ATT_BLOCK = 128
ROPE_BASE = 10000.0
FN_GROUPS = 4
FN_DIM = 64
FN_W = FN_GROUPS * FN_DIM
ODD_SPLITS = (ATT_WQ, ATT_WQ + ATT_WKV, ATT_WQ + 2 * ATT_WKV)

PK_HEADS = 8
PK_NKEYS = 128
PK_DK = 256
PK_DKH = PK_DK // 2
PK_TOPK = 16
PK_CHUNK = 128


def _mm_kernel(a_ref, b_ref, o_ref):
    o_ref[...] = jnp.dot(a_ref[...].astype(jnp.bfloat16), b_ref[...].astype(jnp.bfloat16),
                         preferred_element_type=jnp.float32)


def pmm(a, b, tm=512, tn=512):
    m, k = a.shape
    n = b.shape[1]
    tm = next(t for t in (tm, 256, 128, m) if m % t == 0)
    tn = next(t for t in (tn, 256, 128, n) if n % t == 0)
    return pl.pallas_call(
        _mm_kernel,
        out_shape=jax.ShapeDtypeStruct((m, n), jnp.float32),
        grid=(m // tm, n // tn),
        in_specs=[pl.BlockSpec((tm, k), lambda i, j: (i, 0)),
                  pl.BlockSpec((k, tn), lambda i, j: (0, j))],
        out_specs=pl.BlockSpec((tm, tn), lambda i, j: (i, j)),
        compiler_params=pltpu.CompilerParams(dimension_semantics=("arbitrary", "arbitrary")),
        name="pmm",
    )(a, b)


def pmm3(a, b):
    bsz, t, k = a.shape
    return pmm(a.reshape(bsz * t, k), b).reshape(bsz, t, b.shape[1])


def rms_norm(x, g):
    xf = x.astype(jnp.float32)
    y = xf * lax.rsqrt(jnp.mean(xf * xf, axis=-1, keepdims=True) + EPS)
    return (y * g.astype(jnp.float32)).astype(x.dtype)


def modulate(h, shift, scale):
    return h * (1.0 + scale) + shift


def centred_dwconv(x, w, b):
    left = CONV_W // 2
    y = lax.conv_general_dilated(x, w.astype(x.dtype)[:, None, :], (1,), [(left, CONV_W - 1 - left)],
                                 dimension_numbers=('NWC', 'WIO', 'NWC'), feature_group_count=x.shape[-1])
    return y + b.astype(x.dtype)


HG_SUB = 16
NT_DIMS = (((1,), (1,)), ((), ()))
TN_DIMS = (((0,), (0,)), ((), ()))
HG_QCOL, HG_ICOL, HG_FCOL = 0, HG_HEADS, 2 * HG_HEADS


def _split3(x):
    hi = x.astype(jnp.bfloat16)
    r = x - hi.astype(jnp.float32)
    mid = r.astype(jnp.bfloat16)
    lo = (r - mid.astype(jnp.float32)).astype(jnp.bfloat16)
    return hi, mid, lo


def _dot01(m01, x):
    return sum(jnp.dot(m01, p, preferred_element_type=jnp.float32) for p in _split3(x))


def _hgrn_kernel(qr_ref, ir_ref, fr_ref, lb_ref, s0_ref, o_ref, send_ref, st_scr, *, reverse, n_chunks):
    c = pl.program_id(2)

    @pl.when(c == 0)
    def _():
        st_scr[...] = s0_ref[0, 0]

    n = HG_CHUNK
    row = lax.broadcasted_iota(jnp.int32, (n, n), 0)
    col = lax.broadcasted_iota(jnp.int32, (n, n), 1)
    lb = lb_ref[...]
    f = lb + (1.0 - lb) * jax.nn.sigmoid(fr_ref[0])
    k = 1.0 - f
    g = jnp.log(f)
    q = jax.nn.silu(qr_ref[0])
    v = ir_ref[0]
    if reverse:
        flipm = (row + col == n - 1).astype(jnp.bfloat16)
        q, k, v, g = (_dot01(flipm, a) for a in (q, k, v, g))
    b = _dot01((col <= row).astype(jnp.bfloat16), g)
    st = st_scr[...]
    o = lax.dot_general((q * jnp.exp(b)).astype(jnp.bfloat16), st.astype(jnp.bfloat16), NT_DIMS,
                        preferred_element_type=jnp.float32)
    srow = lax.broadcasted_iota(jnp.int32, (HG_SUB, HG_DK), 0)
    blocks = []
    for r0 in range(0, n, HG_SUB):
        b_i, q_i, k_i, v_i = (a[r0:r0 + HG_SUB] for a in (b, q, k, v))
        o_i = o[r0:r0 + HG_SUB]
        if r0 > 0:
            ref = b[r0 - 1:r0]
            a_q = (q_i * jnp.exp(b_i - ref)).astype(jnp.bfloat16)
            a_k = (k[0:r0] * jnp.exp(ref - b[0:r0])).astype(jnp.bfloat16)
            att = lax.dot_general(a_q, a_k, NT_DIMS, preferred_element_type=jnp.float32)
            o_i = o_i + jnp.dot(att.astype(jnp.bfloat16), v[0:r0].astype(jnp.bfloat16),
                                preferred_element_type=jnp.float32)
        diag = jnp.zeros((HG_SUB, HG_DV), jnp.float32)
        for t in range(HG_SUB):
            e = jnp.exp(jnp.minimum(b_i[t:t + 1] - b_i, 0.0))
            w = jnp.sum(jnp.where(srow <= t, q_i[t:t + 1] * k_i * e, 0.0), axis=1, keepdims=True)
            diag = jnp.where(srow == t, jnp.sum(w * v_i, axis=0, keepdims=True), diag)
        blocks.append(o_i + diag)
    o_full = jnp.concatenate(blocks, axis=0)
    if reverse:
        o_full = _dot01(flipm, o_full)
    o_ref[0] = o_full
    b_end = b[n - 1:n]
    k_dec = (k * jnp.exp(b_end - b)).astype(jnp.bfloat16)
    st_new = st * jnp.exp(b_end) + lax.dot_general(v.astype(jnp.bfloat16), k_dec, TN_DIMS,
                                                   preferred_element_type=jnp.float32)
    st_scr[...] = st_new

    @pl.when(c == n_chunks - 1)
    def _():
        send_ref[0, 0] = st_new


def hgrn_scan(proj, lb, s0t, reverse):
    bsz, t_len, _ = proj.shape
    n_chunks = t_len // HG_CHUNK
    fcol = HG_FCOL + (HG_HEADS if reverse else 0)

    def tmap(c):
        return n_chunks - 1 - c if reverse else c

    blk = (1, HG_CHUNK, HG_DK)
    return pl.pallas_call(
        functools.partial(_hgrn_kernel, reverse=reverse, n_chunks=n_chunks),
        out_shape=(jax.ShapeDtypeStruct((bsz, t_len, HG_HEADS * HG_DV), jnp.float32),
                   jax.ShapeDtypeStruct((bsz, HG_HEADS, HG_DV, HG_DK), jnp.float32)),
        grid=(bsz, HG_HEADS, n_chunks),
        in_specs=[pl.BlockSpec(blk, lambda b, h, c: (b, tmap(c), HG_QCOL + h)),
                  pl.BlockSpec(blk, lambda b, h, c: (b, tmap(c), HG_ICOL + h)),
                  pl.BlockSpec(blk, lambda b, h, c: (b, tmap(c), fcol + h)),
                  pl.BlockSpec((1, HG_DK), lambda b, h, c: (0, h)),
                  pl.BlockSpec((1, 1, HG_DV, HG_DK), lambda b, h, c: (b, h, 0, 0))],
        out_specs=(pl.BlockSpec(blk, lambda b, h, c: (b, tmap(c), h)),
                   pl.BlockSpec((1, 1, HG_DV, HG_DK), lambda b, h, c: (b, h, 0, 0))),
        scratch_shapes=[pltpu.VMEM((HG_DV, HG_DK), jnp.float32)],
        compiler_params=pltpu.CompilerParams(dimension_semantics=("arbitrary", "arbitrary", "arbitrary")),
        name="hgrn_scan",
    )(proj, proj, proj, lb, s0t)


def block_diag(x, w):
    bsz, t_len, _ = x.shape
    y = jnp.einsum('btnc,ncd->btnd', x.reshape(bsz, t_len, RG_BLOCKS, RG_BW), w)
    return y.reshape(bsz, t_len, RG_W)


def rglru_gates(xc, wa, ba, wx, bx, lam):
    r = jax.nn.sigmoid(block_diag(xc, wa) + ba)
    i = jax.nn.sigmoid(block_diag(xc, wx) + bx)
    log_a = -RG_C * r * jax.nn.softplus(-lam.astype(jnp.float32))
    u = jnp.sqrt(-jnp.expm1(2.0 * log_a)) * (i * xc)
    return log_a, u


RG_TB = 256
RG_SUB = 16


def _rglru_kernel(la_ref, u_ref, h0_ref, h_ref, hend_ref, h_scr, *, reverse, n_blocks):
    c = pl.program_id(1)

    @pl.when(c == 0)
    def _():
        h_scr[...] = h0_ref[0]

    row = lax.broadcasted_iota(jnp.int32, (RG_SUB, RG_SUB), 0)
    col = lax.broadcasted_iota(jnp.int32, (RG_SUB, RG_SUB), 1)
    tri = ((col >= row) if reverse else (col <= row)).astype(jnp.bfloat16)
    srow = lax.broadcasted_iota(jnp.int32, (RG_SUB, RG_W), 0)
    n_sub = h_ref.shape[1] // RG_SUB

    def sub_block(j, h_prev):
        jj = n_sub - 1 - j if reverse else j
        r0 = pl.multiple_of(jj * RG_SUB, RG_SUB)
        la = la_ref[0, pl.ds(r0, RG_SUB), :]
        uu = u_ref[0, pl.ds(r0, RG_SUB), :]
        acc_la = _dot01(tri, la)
        out = jnp.zeros((RG_SUB, RG_W), jnp.float32)
        for t in range(RG_SUB):
            seen = (srow >= t) if reverse else (srow <= t)
            e = jnp.exp(jnp.minimum(acc_la[t:t + 1] - acc_la, 0.0))
            h_t = jnp.sum(jnp.where(seen, e * uu, 0.0), axis=0, keepdims=True) + jnp.exp(acc_la[t:t + 1]) * h_prev
            out = jnp.where(srow == t, h_t, out)
        h_ref[0, pl.ds(r0, RG_SUB), :] = out
        last = 0 if reverse else RG_SUB - 1
        return out[last:last + 1]

    h_last = lax.fori_loop(0, n_sub, sub_block, h_scr[...])
    h_scr[...] = h_last

    @pl.when(c == n_blocks - 1)
    def _():
        hend_ref[0] = h_last


def rglru_scan(log_a, u, h0, reverse):
    bsz, t_len, w = log_a.shape
    tb = min(RG_TB, t_len)
    n_blocks = t_len // tb

    def tmap(c):
        return n_blocks - 1 - c if reverse else c

    blk = (1, tb, w)
    return pl.pallas_call(
        functools.partial(_rglru_kernel, reverse=reverse, n_blocks=n_blocks),
        out_shape=(jax.ShapeDtypeStruct((bsz, t_len, w), jnp.float32),
                   jax.ShapeDtypeStruct((bsz, 1, w), jnp.float32)),
        grid=(bsz, n_blocks),
        in_specs=[pl.BlockSpec(blk, lambda b, c: (b, tmap(c), 0)),
                  pl.BlockSpec(blk, lambda b, c: (b, tmap(c), 0)),
                  pl.BlockSpec((1, 1, w), lambda b, c: (b, 0, 0))],
        out_specs=(pl.BlockSpec(blk, lambda b, c: (b, tmap(c), 0)),
                   pl.BlockSpec((1, 1, w), lambda b, c: (b, 0, 0))),
        scratch_shapes=[pltpu.VMEM((1, w), jnp.float32)],
        compiler_params=pltpu.CompilerParams(dimension_semantics=("arbitrary", "arbitrary")),
        name="rglru_scan",
    )(log_a, u, h0)


def rglru_bidir(g_f, g_b, h_f, h_b):
    y_f, h_f = rglru_scan(g_f[0], g_f[1], h_f, False)
    y_b, h_b = rglru_scan(g_b[0], g_b[1], h_b, True)
    return y_f + y_b, h_f, h_b


def even_mixer(hl, hc, w_in, w_out, lb, hg_gain, conv_w, conv_b, wa, ba, wx, bx, lam):
    def prepare(h):
        proj = pmm3(h, w_in)
        _, _, _, _, g, xr, gate = jnp.split(proj, EVEN_SPLITS, axis=-1)
        xc = centred_dwconv(xr, conv_w, conv_b).astype(jnp.float32)
        g_f = rglru_gates(xc, wa[0], ba[0], wx[0], bx[0], lam[0])
        g_b = rglru_gates(xc, wa[1], ba[1], wx[1], bx[1], lam[1])
        return proj, (g_f, g_b), (g, gate)

    def hgrn_bidir(proj, s_f, s_b):
        o_f, s_f = hgrn_scan(proj, lb[0:1], s_f, False)
        o_b, s_b = hgrn_scan(proj, lb[1:2], s_b, True)
        return o_f + o_b, s_f, s_b

    def merge(o_hg, h_rg, g, gate):
        bsz, t_len, _ = g.shape
        o_hg = o_hg.reshape(bsz, t_len, HG_HEADS, HG_DV)
        y_a = rms_norm(o_hg, hg_gain) * jax.nn.silu(g.astype(jnp.float32)).reshape(bsz, t_len, HG_HEADS, HG_DV)
        y_b = jax.nn.gelu(gate.astype(jnp.float32)) * h_rg
        y = jnp.concatenate([y_a.reshape(bsz, t_len, HG_WV), y_b], axis=-1)
        return pmm3(y.astype(g.dtype), w_out)

    hg_c, rg_c, gt_c = prepare(hc)
    hg_l, rg_l, gt_l = prepare(hl)
    bsz = hl.shape[0]
    s0 = jnp.zeros((bsz, HG_HEADS, HG_DK, HG_DV), jnp.float32)
    h0 = jnp.zeros((bsz, 1, RG_W), jnp.float32)
    o_c, s_f, s_b = hgrn_bidir(hg_c, s0, s0)
    r_c, h_f, h_b = rglru_bidir(*rg_c, h0, h0)
    o_l, _, _ = hgrn_bidir(hg_l, s_f, s_b)
    r_l, _, _ = rglru_bidir(*rg_l, h_f, h_b)
    return merge(o_l, r_l, *gt_l), merge(o_c, r_c, *gt_c)


def axial_rope(t_len):
    rows = t_len // GRID_W
    row = jnp.repeat(jnp.arange(rows, dtype=jnp.float32), GRID_W)
    col = jnp.tile(jnp.arange(GRID_W, dtype=jnp.float32), rows)
    n_freq = ATT_DH // 4
    inv = ROPE_BASE ** (-jnp.arange(n_freq, dtype=jnp.float32) / n_freq)
    ang = jnp.concatenate([row[:, None] * inv, col[:, None] * inv], axis=-1)
    return jnp.cos(ang), jnp.sin(ang)


def apply_rope(x, cos, sin):
    xf = x.astype(jnp.float32)
    x1, x2 = jnp.split(xf, 2, axis=-1)
    cs, sn = cos[None, :, None, :], sin[None, :, None, :]
    return jnp.concatenate([x1 * cs - x2 * sn, x2 * cs + x1 * sn], axis=-1).astype(x.dtype)


def sink_attention(q, k, v, sink, mask):
    s = jnp.einsum('bqhgd,bkhd->bhgqk', q, k).astype(jnp.float32) * (ATT_DH ** -0.5)
    if mask is not None:
        s = jnp.where(mask, s, -jnp.inf)
    sk = jnp.broadcast_to(sink.astype(jnp.float32)[None, :, :, None, None], s.shape[:-1] + (1,))
    p = jax.nn.softmax(jnp.concatenate([sk, s], axis=-1), axis=-1)[..., 1:]
    return jnp.einsum('bhgqk,bkhd->bqhgd', p.astype(v.dtype), v)


def windowed_attention(q, k, v, kc, vc, sink):
    bsz, t_len = q.shape[:2]
    n_blk = t_len // ATT_BLOCK
    n_ctx = kc.shape[1]
    pad = ((0, 0), (ATT_BLOCK, ATT_BLOCK), (0, 0), (0, 0))
    kp, vp = jnp.pad(k, pad), jnp.pad(v, pad)
    qb = q.reshape(bsz, n_blk, ATT_BLOCK, ATT_HKV, ATT_G, ATT_DH).transpose(1, 0, 2, 3, 4, 5)
    offs_q = jnp.arange(ATT_BLOCK)
    offs_k = jnp.arange(3 * ATT_BLOCK)
    ctx_mask = jnp.ones((ATT_BLOCK, n_ctx), dtype=bool)

    def block(args):
        n, qn = args
        kn = lax.dynamic_slice_in_dim(kp, n * ATT_BLOCK, 3 * ATT_BLOCK, axis=1)
        vn = lax.dynamic_slice_in_dim(vp, n * ATT_BLOCK, 3 * ATT_BLOCK, axis=1)
        qpos = n * ATT_BLOCK + offs_q
        kpos = (n - 1) * ATT_BLOCK + offs_k
        local = (jnp.abs(qpos[:, None] - kpos[None, :]) <= WINDOW) & (kpos >= 0)[None, :] & (kpos < t_len)[None, :]
        mask = jnp.concatenate([ctx_mask, local], axis=1)
        return sink_attention(qn, jnp.concatenate([kc, kn], axis=1), jnp.concatenate([vc, vn], axis=1), sink, mask)

    o = lax.map(block, (jnp.arange(n_blk), qb))
    return o.transpose(1, 0, 2, 3, 4, 5).reshape(bsz, t_len, ATT_WQ)


def _dft_parts(n):
    k = jnp.arange(n, dtype=jnp.int32)
    ang = ((k[:, None] * k[None, :]) % n).astype(jnp.float32) * (2.0 * np.pi / n)
    return jnp.cos(ang), jnp.sin(ang)


def fourier_mix(z):
    bsz, t_len, _ = z.shape
    c_d, s_d = _dft_parts(FN_DIM)
    eye = jnp.eye(FN_GROUPS, dtype=jnp.float32)
    scale = (t_len * FN_DIM) ** -0.5
    z2 = z.astype(jnp.float32).reshape(bsz * t_len, FN_W)
    c_t, s_t = _dft_parts(t_len)

    def along_t(m_t, zd):
        cols = zd.reshape(bsz, t_len, FN_W).transpose(1, 0, 2).reshape(t_len, bsz * FN_W)
        return pmm(m_t, cols)

    y = along_t(c_t, pmm(z2, jnp.kron(eye, c_d) * scale)) - along_t(s_t, pmm(z2, jnp.kron(eye, s_d) * scale))
    return y.reshape(t_len, bsz, FN_W).transpose(1, 0, 2).astype(z.dtype)


def odd_mixer(hl, hc, w_in, w_out, q_gain, k_gain, sink, need_ctx):
    bsz, t_len, _ = hl.shape
    n_ctx = hc.shape[1]
    ql, kl, vl, fl = jnp.split(pmm3(hl, w_in), ODD_SPLITS, axis=-1)
    cos, sin = axial_rope(t_len)
    ql = apply_rope(rms_norm(ql.reshape(bsz, t_len, ATT_HQ, ATT_DH), q_gain), cos, sin)
    kl = apply_rope(rms_norm(kl.reshape(bsz, t_len, ATT_HKV, ATT_DH), k_gain), cos, sin)
    vl = vl.reshape(bsz, t_len, ATT_HKV, ATT_DH)
    kc, vc = jnp.split(pmm3(hc, w_in[:, ATT_WQ:ATT_WQ + 2 * ATT_WKV]), 2, axis=-1)
    kc = rms_norm(kc.reshape(bsz, n_ctx, ATT_HKV, ATT_DH), k_gain)
    vc = vc.reshape(bsz, n_ctx, ATT_HKV, ATT_DH)
    sink_g = sink.reshape(ATT_HKV, ATT_G)
    a_l = windowed_attention(ql.reshape(bsz, t_len, ATT_HKV, ATT_G, ATT_DH), kl, vl, kc, vc, sink_g)
    y_l = pmm3(jnp.concatenate([a_l, fourier_mix(fl)], axis=-1), w_out)
    if not need_ctx:
        return y_l, None
    qc = rms_norm(pmm3(hc, w_in[:, :ATT_WQ]).reshape(bsz, n_ctx, ATT_HKV, ATT_G, ATT_DH), q_gain)
    fc = pmm3(hc, w_in[:, ATT_WQ + 2 * ATT_WKV:])
    a_c = sink_attention(qc, kc, vc, sink_g, None).reshape(bsz, n_ctx, ATT_WQ)
    y_c = pmm3(jnp.concatenate([a_c, fourier_mix(fc)], axis=-1), w_out)
    return y_l, y_c


PEER_TA = 256
PEER_TG = 8
PEER_GPB = 2
PEER_SUB = 8
PEER_COLS = PK_HEADS * PK_TOPK * PEER_SUB
PEER_CHUNK = 4096
SC_W = 32
SC_IDX_BLOCK = 128


def _topk_rows(s, k):
    r, t = s.shape
    iota = lax.broadcasted_iota(jnp.int32, (r, t), 0)
    kiota = lax.broadcasted_iota(jnp.int32, (k, t), 0)
    vals = jnp.zeros((k, t), jnp.float32)
    rows = jnp.zeros((k, t), jnp.int32)
    for j in range(k):
        m = jnp.max(s, axis=0, keepdims=True)
        am = jnp.min(jnp.where(s == m, iota, r), axis=0, keepdims=True)
        vals = jnp.where(kiota == j, m, vals)
        rows = jnp.where(kiota == j, am, rows)
        s = jnp.where(iota == am, -jnp.inf, s)
    return vals, rows


def _route_kernel(x_ref, wq_ref, bq_ref, keys_ref, eidx_ref, gate_ref):
    xb = x_ref[...].astype(jnp.bfloat16)

    def head(h, carry):
        q = jnp.dot(xb, wq_ref[h], preferred_element_type=jnp.float32) + bq_ref[h]
        sv, si = [], []
        for p in range(2):
            qs = q[:, p * PK_DKH:(p + 1) * PK_DKH].astype(jnp.bfloat16)
            s_t = lax.dot_general(keys_ref[h, p], qs, NT_DIMS, preferred_element_type=jnp.float32)
            v, i = _topk_rows(s_t, PK_TOPK)
            sv.append(v)
            si.append(i)
        cand = [sv[0][0:1] + sv[1]]
        cidx = [si[0][0:1] * PK_NKEYS + si[1]]
        for a in range(1, 8):
            cand.append(sv[0][a:a + 1] + sv[1][0:8])
            cidx.append(si[0][a:a + 1] * PK_NKEYS + si[1][0:8])
        cand.append(sv[0][8:16] + sv[1][0:1])
        cidx.append(si[0][8:16] * PK_NKEYS + si[1][0:1])
        cand = jnp.concatenate(cand, axis=0)
        cidx = jnp.concatenate(cidx, axis=0)
        best, pos = _topk_rows(cand, PK_TOPK)
        r, t = cand.shape
        iota = lax.broadcasted_iota(jnp.int32, (r, t), 0)
        kiota = lax.broadcasted_iota(jnp.int32, (PK_TOPK, t), 0)
        eidx = jnp.zeros((PK_TOPK, t), jnp.int32)
        for j in range(PK_TOPK):
            e = jnp.max(jnp.where(iota == pos[j:j + 1], cidx, -1), axis=0, keepdims=True)
            eidx = jnp.where(kiota == j, e, eidx)
        ex = jnp.exp(best - best[0:1])
        eidx_ref[h] = eidx
        gate_ref[h] = ex / jnp.sum(ex, axis=0, keepdims=True)
        return carry

    lax.fori_loop(0, PK_HEADS, head, 0)


def peer_route(x, wq3, bq3, keys_b):
    n, d = x.shape
    return pl.pallas_call(
        _route_kernel,
        out_shape=(jax.ShapeDtypeStruct((PK_HEADS, PK_TOPK, n), jnp.int32),
                   jax.ShapeDtypeStruct((PK_HEADS, PK_TOPK, n), jnp.float32)),
        grid=(n // PEER_TA,),
        in_specs=[pl.BlockSpec((PEER_TA, d), lambda i: (i, 0)),
                  pl.BlockSpec((PK_HEADS, d, PK_DK), lambda i: (0, 0, 0)),
                  pl.BlockSpec((PK_HEADS, 1, PK_DK), lambda i: (0, 0, 0)),
                  pl.BlockSpec((PK_HEADS, 2, PK_NKEYS, PK_DKH), lambda i: (0, 0, 0, 0))],
        out_specs=(pl.BlockSpec((PK_HEADS, PK_TOPK, PEER_TA), lambda i: (0, 0, i)),
                   pl.BlockSpec((PK_HEADS, PK_TOPK, PEER_TA), lambda i: (0, 0, i))),
        compiler_params=pltpu.CompilerParams(dimension_semantics=("arbitrary",)),
        name="peer_route",
    )(x, wq3, bq3, keys_b)


def _lane_group_allsum(y, lane):
    k = 1
    while k < PEER_SUB:
        up = pltpu.roll(y, 128 - k, axis=1)
        dn = pltpu.roll(y, k, axis=1)
        y = y + jnp.where((lane & k) == 0, up, dn)
        k *= 2
    return y


def _mix_kernel(x2_ref, gate_ref, g_ref, o2_ref, hp_ref):
    lane = lax.broadcasted_iota(jnp.int32, (PEER_SUB, PEER_COLS), 1)
    sub = lax.broadcasted_iota(jnp.int32, (PEER_SUB, PEER_COLS), 0)
    same_s = (lane % PEER_SUB) == sub
    lane128 = lax.broadcasted_iota(jnp.int32, (PEER_TG, 128), 1)
    for g in range(PEER_GPB):
        for t in range(PEER_TG):
            tok = g * PEER_TG + t
            w = g_ref[pl.ds(tok * PEER_COLS, PEER_COLS), :]
            ub = lax.bitcast_convert_type(w << 16, jnp.float32).astype(jnp.bfloat16)
            x2 = x2_ref[pl.ds(tok * PEER_SUB, PEER_SUB), :].astype(jnp.bfloat16)
            p = lax.dot_general(x2, ub, NT_DIMS, preferred_element_type=jnp.float32)
            hp_ref[pl.ds(t, 1), :] = jnp.sum(jnp.where(same_s, p, 0.0), axis=0, keepdims=True)
        hp = hp_ref[...]
        hfull = jnp.concatenate(
            [_lane_group_allsum(hp[:, c * 128:(c + 1) * 128], lane128) for c in range(PEER_COLS // 128)], axis=1)
        act = jax.nn.gelu(hfull) * gate_ref[pl.ds(g * PEER_TG, PEER_TG), :]
        for t in range(PEER_TG):
            tok = g * PEER_TG + t
            a2 = jnp.where(same_s, act[t:t + 1, :], 0.0).astype(jnp.bfloat16)
            w = g_ref[pl.ds(tok * PEER_COLS, PEER_COLS), :]
            vb = lax.bitcast_convert_type(w & jnp.int32(-65536), jnp.float32).astype(jnp.bfloat16)
            o2_ref[pl.ds(tok * PEER_SUB, PEER_SUB), :] = jnp.dot(a2, vb, preferred_element_type=jnp.float32)


def peer_mix(x, gate, gathered):
    n, d = x.shape
    tb = PEER_TG * PEER_GPB
    x2 = x.reshape(n * PEER_SUB, 128)
    gate8 = jnp.repeat(gate, PEER_SUB, axis=1)
    g2 = gathered.reshape(n * PEER_COLS, 128)
    out2 = pl.pallas_call(
        _mix_kernel,
        out_shape=jax.ShapeDtypeStruct((n * PEER_SUB, 128), jnp.float32),
        grid=(n // tb,),
        in_specs=[pl.BlockSpec((tb * PEER_SUB, 128), lambda i: (i, 0)),
                  pl.BlockSpec((tb, PEER_COLS), lambda i: (i, 0)),
                  pl.BlockSpec((tb * PEER_COLS, 128), lambda i: (i, 0))],
        out_specs=pl.BlockSpec((tb * PEER_SUB, 128), lambda i: (i, 0)),
        scratch_shapes=[pltpu.VMEM((PEER_TG, PEER_COLS), jnp.float32)],
        compiler_params=pltpu.CompilerParams(dimension_semantics=("arbitrary",),
                                             vmem_limit_bytes=48 * 1024 * 1024),
        name="peer_mix",
    )(x2, gate8, g2)
    return out2.reshape(n, d)


def sc_gather(table, idxp):
    nblk = idxp.shape[1] // SC_IDX_BLOCK
    mesh = plsc.VectorSubcoreMesh(core_axis_name="c", subcore_axis_name="s")

    @functools.partial(pl.kernel, out_type=jax.ShapeDtypeStruct((nblk * SC_W, 8, 128), table.dtype), mesh=mesh)
    def gather_kernel(t_hbm, i_hbm, o_hbm):
        def body(i_vmem, o_vmem):
            pltpu.sync_copy(t_hbm.at[i_vmem.at[0, pl.ds(0, SC_W)]], o_vmem)

        pltpu.emit_pipeline(
            body, grid=(nblk,),
            in_specs=[pl.BlockSpec((1, SC_IDX_BLOCK), lambda i: (0, i))],
            out_specs=[pl.BlockSpec((SC_W, 8, 128), lambda i: (i, 0, 0))],
            core_axis_name=("c", "s"),
            dimension_semantics=(pltpu.PARALLEL,),
            trace_scopes=False,
        )(i_hbm, o_hbm)

    return gather_kernel(table, idxp)


def pack_table(u, v):
    ub = lax.bitcast_convert_type(u.astype(jnp.bfloat16), jnp.uint16).astype(jnp.uint32)
    vb = lax.bitcast_convert_type(v.astype(jnp.bfloat16), jnp.uint16).astype(jnp.uint32)
    w = lax.bitcast_convert_type(ub | (vb << 16), jnp.int32)
    return w.reshape(u.shape[0], 8, 128)


def peer(h, w_q, b_q, sub_keys, table):
    n, d = h.shape
    wq3 = w_q.reshape(d, PK_HEADS, PK_DK).transpose(1, 0, 2).astype(jnp.bfloat16)
    eidx, gate = peer_route(h, wq3, b_q.reshape(PK_HEADS, 1, PK_DK), sub_keys.astype(jnp.bfloat16))
    eidx = eidx.reshape(PK_HEADS * PK_TOPK, n).T
    gate = gate.reshape(PK_HEADS * PK_TOPK, n).T
    outs = []
    for c0 in range(0, n, PEER_CHUNK):
        c1 = min(n, c0 + PEER_CHUNK)
        e = eidx[c0:c1].reshape(-1, SC_W)
        idxp = jnp.pad(e, ((0, 0), (0, SC_IDX_BLOCK - SC_W))).reshape(1, -1)
        gathered = sc_gather(table, idxp)
        outs.append(peer_mix(h[c0:c1], gate[c0:c1], gathered))
    return jnp.concatenate(outs, axis=0)


def kernel(x, c, ctx, c_ctx, w_mod, b_mod, g_mix, g_ffn, e_w_in, e_w_out, hg_lb_logits, hg_gain, rg_conv_w,
           rg_conv_b, rg_wa, rg_ba, rg_wx, rg_bx, rg_lambda, o_w_in, o_w_out, q_gain, k_gain, sinks, p_wq,
           p_bq, p_keys, p_u, p_v):
    lb_all = jnp.cumsum(jax.nn.softmax(hg_lb_logits.astype(jnp.float32), axis=1), axis=1)
    tables = [pack_table(p_u[l], p_v[l]) for l in range(DEPTH)]
    lat, cx = x, ctx
    for l in range(DEPTH):
        last = l == DEPTH - 1
        j = l // 2
        m_lat = jnp.split((jax.nn.silu(c) @ w_mod[l] + b_mod[l])[:, None, :], N_MOD, axis=-1)
        m_ctx = jnp.split(jax.nn.silu(c_ctx) @ w_mod[l] + b_mod[l], N_MOD, axis=-1)
        hl = modulate(rms_norm(lat, g_mix[l]), m_lat[0], m_lat[1])
        hc = modulate(rms_norm(cx, g_mix[l]), m_ctx[0], m_ctx[1])
        if l % 2 == 0:
            yl, yc = even_mixer(hl, hc, e_w_in[j], e_w_out[j], lb_all[:, j], hg_gain[j], rg_conv_w[j], rg_conv_b[j],
                                rg_wa[j], rg_ba[j], rg_wx[j], rg_bx[j], rg_lambda[j])
        else:
            yl, yc = odd_mixer(hl, hc, o_w_in[j], o_w_out[j], q_gain[j], k_gain[j], sinks[j], not last)
        lat = lat + m_lat[2] * yl
        hl2 = modulate(rms_norm(lat, g_ffn[l]), m_lat[3], m_lat[4])
        n_lat = hl2.shape[0] * hl2.shape[1]
        if last:
            ff = peer(hl2.reshape(n_lat, hl2.shape[-1]), p_wq[l], p_bq[l], p_keys[l], tables[l])
            lat = lat + m_lat[5] * ff.reshape(lat.shape)
        else:
            cx = cx + m_ctx[2] * yc
            hc2 = modulate(rms_norm(cx, g_ffn[l]), m_ctx[3], m_ctx[4])
            tokens = jnp.concatenate([hl2.reshape(n_lat, hl2.shape[-1]), hc2.reshape(-1, hc2.shape[-1])], axis=0)
            ff = peer(tokens, p_wq[l], p_bq[l], p_keys[l], tables[l])
            lat = lat + m_lat[5] * ff[:n_lat].reshape(lat.shape)
            cx = cx + m_ctx[5] * ff[n_lat:].reshape(cx.shape)
    return lat
```
